```python
import math, functools
import jax, jax.numpy as jnp
from jax import lax
import numpy as np

D_MODEL = 2048
BATCH = 16
SEQ = 2048
DEPTH = 1
DEC_BATCH = 32
DEC_SEQ = 4
PAST_LEN = 16384
PAGE_SIZE = 128

N_META = 16
N_HEADS = 8
HEAD_DIM = 64
V_DIM = 2 * HEAD_DIM
ATTN_WIDTH = N_HEADS * 2 * HEAD_DIM
V_WIDTH = N_HEADS * V_DIM
LRU_WIDTH = D_MODEL // 2
N_LRU_BLOCKS = 8
LRU_BLOCK = LRU_WIDTH // N_LRU_BLOCKS
CONV_WIDTH = 4
LRU_C = 8.0
D_FF = 4 * D_MODEL
Q_BLOCK = 128
NEG = -1e30
EPS = 1e-6

kernel_name = 'hybrid_diffattn_rglru_decode_step'

F32 = jnp.float32


def rmsnorm(x, g):
    xf = x.astype(F32)
    y = xf * lax.rsqrt(jnp.mean(xf * xf, axis=-1, keepdims=True) + EPS)
    return (y * g.astype(F32)).astype(x.dtype)


def causal_conv(x, buf, w, b):
    T = x.shape[1]
    xp = jnp.concatenate([buf.astype(x.dtype), x], axis=1)
    y = b.astype(x.dtype) + xp[:, 0:T] * w[0]
    for j in range(1, CONV_WIDTH):
        y = y + xp[:, j:j + T] * w[j]
    return y, xp[:, -(CONV_WIDTH - 1):]


def rg_lru(x, h0, w_rg, b_rg, w_ig, b_ig, lam, seq_start):
    B, T, W = x.shape
    xb = x.reshape(B, T, N_LRU_BLOCKS, LRU_BLOCK)
    r = jax.nn.sigmoid((jnp.einsum('btnc,ncd->btnd', xb, w_rg).reshape(B, T, W) + b_rg).astype(F32))
    i = jax.nn.sigmoid((jnp.einsum('btnc,ncd->btnd', xb, w_ig).reshape(B, T, W) + b_ig).astype(F32))
    log_a = -LRU_C * r * jax.nn.softplus(-lam.astype(F32))
    a = jnp.exp(log_a)
    mult = jnp.sqrt(-jnp.expm1(2.0 * log_a))
    if seq_start:
        mult = mult.at[:, 0].set(1.0)
    u = mult * i * x.astype(F32)

    def combine(lhs, rhs):
        return (lhs[0] * rhs[0], rhs[0] * lhs[1] + rhs[1])

    a_cum, u_cum = lax.associative_scan(combine, (a, u), axis=1)
    h = a_cum * h0.astype(F32)[:, None] + u_cum
    return h.astype(x.dtype), h[:, -1].astype(h0.dtype)


def diff_attn_prompt(q, k, v, lam):
    B, T = q.shape[:2]
    nqb = -(-T // Q_BLOCK)
    pad = nqb * Q_BLOCK - T
    qp = jnp.pad(q, ((0, 0), (0, pad), (0, 0), (0, 0), (0, 0)))
    qb = qp.reshape(B, nqb, Q_BLOCK, N_HEADS, 2, HEAD_DIM).transpose(1, 0, 2, 3, 4, 5)
    kpos = jnp.arange(T)
    scale = HEAD_DIM ** -0.5
    vf = v.astype(F32)

    def block(args):
        qi, bi = args
        qpos = bi * Q_BLOCK + jnp.arange(Q_BLOCK)
        s = jnp.einsum('bqhcd,bkhcd->bhcqk', qi, k, preferred_element_type=F32) * scale
        s = jnp.where(kpos[None, :] <= qpos[:, None], s, NEG)
        p = jax.nn.softmax(s, axis=-1)
        attn = p[:, :, 0] - lam * p[:, :, 1]
        return jnp.einsum('bhqk,bkhe->bqhe', attn, vf)

    out = lax.map(block, (qb, jnp.arange(nqb)))
    out = out.transpose(1, 0, 2, 3, 4).reshape(B, nqb * Q_BLOCK, N_HEADS, V_DIM)
    return out[:, :T]


def diff_attn_sample(q, k, v, lam, cache_k, cache_v, page_table):
    DB, S = q.shape[:2]
    scale = HEAD_DIM ** -0.5
    qf = q.astype(F32)

    def scores(kk):
        return jnp.einsum('bqhcd,bkhcd->bhcqk', qf, kk.astype(F32)) * scale

    causal = jnp.tril(jnp.ones((S, S), bool))
    s = jnp.where(causal, scores(k), NEG)
    m = jnp.max(s, axis=-1)
    p = jnp.exp(s - m[..., None])
    l = jnp.sum(p, axis=-1)
    o = jnp.einsum('bhcqk,bkhe->bhcqe', p, v.astype(F32))

    def step(carry, pages):
        m, l, o = carry
        kp = cache_k[pages].reshape(DB, PAGE_SIZE, N_HEADS, 2, HEAD_DIM)
        vp = cache_v[pages].astype(F32)
        s = scores(kp)
        m_new = jnp.maximum(m, jnp.max(s, axis=-1))
        alpha = jnp.exp(m - m_new)
        p = jnp.exp(s - m_new[..., None])
        l = l * alpha + jnp.sum(p, axis=-1)
        o = o * alpha[..., None] + jnp.einsum('bhcqk,bkhe->bhcqe', p, vp)
        return (m_new, l, o), None

    (m, l, o), _ = lax.scan(step, (m, l, o), page_table.T)
    o = o / l[..., None]
    attn = o[:, :, 0] - lam * o[:, :, 1]
    return attn.transpose(0, 2, 1, 3)


def trunk_layer(x, conv_buf, lru_h0, attend, seq_start, lambda_init, p):
    B, T, _ = x.shape
    h = rmsnorm(x, p['g_pre_mix'])
    z = h @ p['w_in']
    sizes = [ATTN_WIDTH, ATTN_WIDTH, V_WIDTH, LRU_WIDTH, LRU_WIDTH, D_MODEL, D_MODEL]
    splits = np.cumsum(sizes)[:-1].tolist()
    q, k, v, xr, gr, ga, gl = jnp.split(z, splits, axis=-1)
    q = q.reshape(B, T, N_HEADS, 2, HEAD_DIM)
    k = k.reshape(B, T, N_HEADS, 2, HEAD_DIM)
    v = v.reshape(B, T, N_HEADS, V_DIM)
    lam = (jnp.exp(jnp.sum(p['lam_q1'].astype(F32) * p['lam_k1'].astype(F32)))
           - jnp.exp(jnp.sum(p['lam_q2'].astype(F32) * p['lam_k2'].astype(F32))) + lambda_init)
    o = attend(q, k, v, lam)
    o = rmsnorm(o, p['g_subln']) * (1.0 - lambda_init)
    y_attn = o.reshape(B, T, V_WIDTH).astype(x.dtype) @ p['w_attn_br']
    xc, new_conv = causal_conv(xr, conv_buf, p['conv_w'], p['conv_b'])
    hr, h_last = rg_lru(xc, lru_h0, p['w_rg'], p['b_rg'], p['w_ig'], p['b_ig'], p['lru_lambda'], seq_start)
    y_lru = (hr * jax.nn.gelu(gr)) @ p['w_lru_br']
    merged = jax.nn.sigmoid(ga) * y_attn + jax.nn.sigmoid(gl) * y_lru
    x = x + rmsnorm(merged @ p['w_out'], p['g_post_mix'])
    u = jnp.square(jax.nn.relu(rmsnorm(x, p['g_pre_mlp']) @ p['w_up']))
    x = x + rmsnorm(u @ p['w_down'], p['g_post_mlp'])
    return x, k.reshape(B, T, N_HEADS, 2 * HEAD_DIM), v, h_last, new_conv


def setup_inputs(seed: int = 0) -> dict:
    key = jax.random.key(seed)
    ks = jax.random.split(key, 32)
    n_pages = PAST_LEN // PAGE_SIZE
    n_used = DEC_BATCH * n_pages
    n_phys = n_used + max(n_used // 4, 1)
    in_width = 2 * ATTN_WIDTH + V_WIDTH + 2 * LRU_WIDTH + 2 * D_MODEL

    def nrm(k, shape, s):
        return jax.random.normal(k, shape, F32) * s

    def gain(k, shape):
        return 1.0 + 0.05 * jax.random.normal(k, shape, F32)

    page_table = jax.random.permutation(ks[6], n_phys)[:n_used].reshape(DEC_BATCH, n_pages).astype(jnp.int32)
    a_c = jax.random.uniform(ks[20], (DEPTH, LRU_WIDTH), F32, 0.9, 0.999)
    s = a_c ** (1.0 / LRU_C)
    lru_lambda = jnp.log(s) - jnp.log1p(-s)
    return {
        'x_prompt': nrm(ks[0], (BATCH, SEQ, D_MODEL), 1.0),
        'x_sample': nrm(ks[1], (DEC_BATCH, DEC_SEQ, D_MODEL), 1.0),
        'cache_k': nrm(ks[2], (DEPTH, n_phys, PAGE_SIZE, N_HEADS, 2 * HEAD_DIM), 1.0),
        'cache_v': nrm(ks[3], (DEPTH, n_phys, PAGE_SIZE, N_HEADS, V_DIM), 1.0),
        'state_lru': nrm(ks[4], (DEPTH, DEC_BATCH, LRU_WIDTH), 0.5),
        'state_conv': nrm(ks[5], (DEPTH, DEC_BATCH, CONV_WIDTH - 1, LRU_WIDTH), 1.0),
        'page_table': page_table,
        'meta_tokens': nrm(ks[7], (N_META, D_MODEL), 1.0),
        'g_pre_mix': gain(ks[8], (DEPTH, D_MODEL)),
        'w_in': nrm(ks[9], (DEPTH, D_MODEL, in_width), D_MODEL ** -0.5),
        'lam_q1': nrm(ks[10], (DEPTH, HEAD_DIM), 0.1),
        'lam_k1': nrm(ks[11], (DEPTH, HEAD_DIM), 0.1),
        'lam_q2': nrm(ks[12], (DEPTH, HEAD_DIM), 0.1),
        'lam_k2': nrm(ks[13], (DEPTH, HEAD_DIM), 0.1),
        'g_subln': gain(ks[14], (DEPTH, V_DIM)),
        'w_attn_br': nrm(ks[15], (DEPTH, V_WIDTH, D_MODEL), V_WIDTH ** -0.5),
        'conv_w': nrm(ks[16], (DEPTH, CONV_WIDTH, LRU_WIDTH), CONV_WIDTH ** -0.5),
        'conv_b': nrm(ks[17], (DEPTH, LRU_WIDTH), 0.02),
        'w_rg': nrm(ks[18], (DEPTH, N_LRU_BLOCKS, LRU_BLOCK, LRU_BLOCK), LRU_BLOCK ** -0.5),
        'b_rg': nrm(ks[19], (DEPTH, LRU_WIDTH), 0.02),
        'w_ig': nrm(ks[21], (DEPTH, N_LRU_BLOCKS, LRU_BLOCK, LRU_BLOCK), LRU_BLOCK ** -0.5),
        'b_ig': nrm(ks[22], (DEPTH, LRU_WIDTH), 0.02),
        'lru_lambda': lru_lambda,
        'w_lru_br': nrm(ks[23], (DEPTH, LRU_WIDTH, D_MODEL), LRU_WIDTH ** -0.5),
        'w_out': nrm(ks[24], (DEPTH, D_MODEL, D_MODEL), D_MODEL ** -0.5),
        'g_post_mix': gain(ks[25], (DEPTH, D_MODEL)),
        'g_pre_mlp': gain(ks[26], (DEPTH, D_MODEL)),
        'w_up': nrm(ks[27], (DEPTH, D_MODEL, D_FF), D_MODEL ** -0.5),
        'w_down': nrm(ks[28], (DEPTH, D_FF, D_MODEL), D_FF ** -0.5),
        'g_post_mlp': gain(ks[29], (DEPTH, D_MODEL)),
    }


def reference(x_prompt, x_sample, cache_k, cache_v, state_lru, state_conv, page_table, meta_tokens,
              g_pre_mix, w_in, lam_q1, lam_k1, lam_q2, lam_k2, g_subln, w_attn_br, conv_w, conv_b,
              w_rg, b_rg, w_ig, b_ig, lru_lambda, w_lru_br, w_out, g_post_mix, g_pre_mlp, w_up,
              w_down, g_post_mlp):
    B = x_prompt.shape[0]
    meta = jnp.broadcast_to(meta_tokens.astype(x_prompt.dtype)[None], (B, N_META, D_MODEL))
    xp = jnp.concatenate([meta, x_prompt], axis=1)
    xs = x_sample
    kp_l, vp_l, hp_l, cp_l, ks_l, vs_l, hs_l, cs_l = [], [], [], [], [], [], [], []
    for l in range(DEPTH):
        lp = dict(g_pre_mix=g_pre_mix[l], w_in=w_in[l], lam_q1=lam_q1[l], lam_k1=lam_k1[l],
                  lam_q2=lam_q2[l], lam_k2=lam_k2[l], g_subln=g_subln[l], w_attn_br=w_attn_br[l],
                  conv_w=conv_w[l], conv_b=conv_b[l], w_rg=w_rg[l], b_rg=b_rg[l], w_ig=w_ig[l],
                  b_ig=b_ig[l], lru_lambda=lru_lambda[l], w_lru_br=w_lru_br[l], w_out=w_out[l],
                  g_post_mix=g_post_mix[l], g_pre_mlp=g_pre_mlp[l], w_up=w_up[l], w_down=w_down[l],
                  g_post_mlp=g_post_mlp[l])
        lambda_init = 0.8 - 0.6 * math.exp(-0.3 * l)
        xp, kp, vp, hp, cp = trunk_layer(
            xp, jnp.zeros((B, CONV_WIDTH - 1, LRU_WIDTH), xp.dtype), jnp.zeros((B, LRU_WIDTH), xp.dtype),
            diff_attn_prompt, True, lambda_init, lp)
        attend_s = functools.partial(diff_attn_sample, cache_k=cache_k[l], cache_v=cache_v[l],
                                     page_table=page_table)
        xs, ks, vs, hs, cs = trunk_layer(xs, state_conv[l], state_lru[l], attend_s, False, lambda_init, lp)
        kp_l.append(kp); vp_l.append(vp); hp_l.append(hp); cp_l.append(cp)
        ks_l.append(ks); vs_l.append(vs); hs_l.append(hs); cs_l.append(cs)
    y_prompt = xp[:, N_META:]
    y_sample = xs
    k_prompt = jnp.stack(kp_l); v_prompt = jnp.stack(vp_l)
    lru_prompt = jnp.stack(hp_l); conv_prompt = jnp.stack(cp_l)
    k_sample = jnp.stack(ks_l); v_sample = jnp.stack(vs_l)
    lru_sample = jnp.stack(hs_l); conv_sample = jnp.stack(cs_l)
    return (y_prompt, y_sample, k_prompt, v_prompt, lru_prompt, conv_prompt, k_sample, v_sample, lru_sample, conv_sample)
```

```python
import functools
import math

import jax
import jax.numpy as jnp
from jax import lax
from jax.experimental import pallas as pl
from jax.experimental.pallas import tpu as pltpu

F32 = jnp.float32
BF16 = jnp.bfloat16

D_MODEL = 2048
N_META = 16
N_HEADS = 8
HEAD_DIM = 64
V_DIM = 2 * HEAD_DIM
ATTN_WIDTH = N_HEADS * 2 * HEAD_DIM
V_WIDTH = N_HEADS * V_DIM
LRU_WIDTH = D_MODEL // 2
N_LRU_BLOCKS = 8
LRU_BLOCK = LRU_WIDTH // N_LRU_BLOCKS
CONV_WIDTH = 4
LRU_C = 8.0
D_FF = 4 * D_MODEL
PAGE_SIZE = 128
NEG = -1e30
EPS = 1e-6
LAMBDA_INIT = 0.8 - 0.6 * math.exp(-0.3 * 0)
IN_WIDTH = 2 * ATTN_WIDTH + V_WIDTH + 2 * LRU_WIDTH + 2 * D_MODEL

SUBLANES = 8
LANES = 128
V7X_VMEM_CAP_BYTES = 60 * 1024 * 1024

_IN_SEGMENTS = ((ATTN_WIDTH, BF16), (ATTN_WIDTH, F32), (V_WIDTH, F32), (LRU_WIDTH, F32), (LRU_WIDTH, F32),
                (2 * D_MODEL, F32))


def _nbytes(shape, dtype):
    return math.prod(shape) * jnp.dtype(dtype).itemsize


def _params(semantics, vmem_bytes):
    return pltpu.CompilerParams(dimension_semantics=semantics,
                                vmem_limit_bytes=min(int(vmem_bytes), V7X_VMEM_CAP_BYTES))


def _rms(x, g):
    return x * lax.rsqrt(jnp.mean(x * x, axis=-1, keepdims=True) + EPS) * g


def _diff_lambda(lq1_ref, lk1_ref, lq2_ref, lk2_ref):
    a = jnp.sum(lq1_ref[...] * lk1_ref[...], axis=1, keepdims=True)
    b = jnp.sum(lq2_ref[...] * lk2_ref[...], axis=1, keepdims=True)
    return jnp.exp(a) - jnp.exp(b) + LAMBDA_INIT


def _in_proj_kernel(x_ref, g_ref, w_ref, *rest, seg_steps):
    out_refs, h_scr = rest[:-1], rest[-1]
    n = pl.program_id(1)

    @pl.when(n == 0)
    def _():
        h_scr[...] = _rms(x_ref[...], g_ref[...]).astype(BF16)

    z = jnp.dot(h_scr[...], w_ref[...], preferred_element_type=F32)
    for idx, (o_ref, (lo, hi)) in enumerate(zip(out_refs, seg_steps)):
        @pl.when((n >= lo) & (n < hi))
        def _(o_ref=o_ref, idx=idx):
            val = z * (HEAD_DIM ** -0.5) if idx == 0 else z
            o_ref[...] = val.astype(o_ref.dtype)


def _in_proj(x, g, w, *, tm, tn):
    m_rows = x.shape[0]
    assert m_rows % tm == 0 and all(wd % tn == 0 for wd, _ in _IN_SEGMENTS)
    seg_steps, out_specs, out_shapes = [], [], []
    lo = 0
    for wd, dt in _IN_SEGMENTS:
        cnt = wd // tn
        seg_steps.append((lo, lo + cnt))
        out_specs.append(pl.BlockSpec((tm, tn), lambda m, n, lo=lo, cnt=cnt: (m, jnp.clip(n - lo, 0, cnt - 1))))
        out_shapes.append(jax.ShapeDtypeStruct((m_rows, wd), dt))
        lo += cnt
    vmem = (2 * _nbytes((tm, D_MODEL), F32) + _nbytes((tm, D_MODEL), BF16) + 2 * _nbytes((D_MODEL, tn), BF16)
            + sum(2 * _nbytes((tm, tn), dt) for _, dt in _IN_SEGMENTS) + 3 * _nbytes((tm, tn), F32)
            + _nbytes((tm, D_MODEL), F32))
    return pl.pallas_call(
        functools.partial(_in_proj_kernel, seg_steps=tuple(seg_steps)),
        grid=(m_rows // tm, IN_WIDTH // tn),
        in_specs=[pl.BlockSpec((tm, D_MODEL), lambda m, n: (m, 0)),
                  pl.BlockSpec((1, D_MODEL), lambda m, n: (0, 0)),
                  pl.BlockSpec((D_MODEL, tn), lambda m, n: (0, n))],
        out_specs=out_specs,
        out_shape=out_shapes,
        scratch_shapes=[pltpu.VMEM((tm, D_MODEL), BF16)],
        compiler_params=_params(("parallel", "arbitrary"), vmem),
        name="in_proj",
    )(x, g, w)


def _lru_kernel(xr_ref, gr_ref, cbuf_ref, h0_ref, cw_ref, cb_ref, wrg_ref, brg_ref, wig_ref, big_ref, lam_ref,
                hg_ref, hlast_ref, cnew_ref, ext_scr, a_scr, u_scr, h_scr, *, tt, seq_start, t_last):
    c = pl.program_id(1)

    @pl.when(c == 0)
    def _():
        ext_scr[0:SUBLANES, :] = cbuf_ref[0]
        h_scr[...] = h0_ref[0]

    xr = xr_ref[0]
    ext_scr[SUBLANES:SUBLANES + tt, :] = xr
    cw = cw_ref[...]
    xc = (cb_ref[...] + ext_scr[5:5 + tt, :] * cw[0:1] + ext_scr[6:6 + tt, :] * cw[1:2]
          + ext_scr[7:7 + tt, :] * cw[2:3] + xr * cw[3:4])
    ext_scr[0:SUBLANES, :] = ext_scr[tt:tt + SUBLANES, :]

    xcb = xc.astype(BF16)

    def gate(w_ref, b_ref):
        parts = [jnp.dot(xcb[:, n * LRU_BLOCK:(n + 1) * LRU_BLOCK], w_ref[n], preferred_element_type=F32)
                 for n in range(N_LRU_BLOCKS)]
        return jax.nn.sigmoid(jnp.concatenate(parts, axis=1) + b_ref[...])

    r = gate(wrg_ref, brg_ref)
    i = gate(wig_ref, big_ref)
    lam = lam_ref[...]
    softplus_neg_lam = jnp.maximum(-lam, 0.0) + jnp.log1p(jnp.exp(-jnp.abs(lam)))
    log_a = (-LRU_C) * r * softplus_neg_lam
    a = jnp.exp(log_a)
    mult = jnp.sqrt(-jnp.tanh(log_a) * (a * a + 1.0))
    row = lax.broadcasted_iota(jnp.int32, (tt, LRU_WIDTH), 0)
    if seq_start:
        mult = jnp.where(row + c * tt == 0, 1.0, mult)
    u = mult * i * xc

    rmod = row & (SUBLANES - 1)
    for s in (1, 2, 4):
        ok = rmod >= s
        a_prev = pltpu.roll(a, s, axis=0)
        u_prev = pltpu.roll(u, s, axis=0)
        u = jnp.where(ok, a * u_prev + u, u)
        a = jnp.where(ok, a * a_prev, a)
    a_scr[...] = a
    u_scr[...] = u

    def carry(gidx, h):
        rows = pl.ds(pl.multiple_of(gidx * SUBLANES, SUBLANES), SUBLANES)
        hrows = a_scr[rows, :] * h + u_scr[rows, :]
        u_scr[rows, :] = hrows
        return hrows[SUBLANES - 1:SUBLANES, :]

    h_scr[...] = lax.fori_loop(0, tt // SUBLANES, carry, h_scr[...])
    hg_ref[0] = (u_scr[...] * jax.nn.gelu(gr_ref[0])).astype(BF16)

    @pl.when(c == pl.num_programs(1) - 1)
    def _():
        hlast_ref[0] = u_scr[t_last:t_last + 1, :]
        cnew_ref[0] = ext_scr[SUBLANES + t_last - 2:SUBLANES + t_last + 1, :]


def _lru(xr, gr, cbuf, h0, lp, *, tt, seq_start, t_valid):
    b, t, w = xr.shape
    assert t % tt == 0 and tt % SUBLANES == 0 and (t_valid - 1) // tt == t // tt - 1 and t_valid >= CONV_WIDTH - 1
    t_last = (t_valid - 1) % tt
    assert t_last >= CONV_WIDTH - 2
    seq = lambda bi, c: (bi, c, 0)
    per_b = lambda bi, c: (bi, 0, 0)
    const2 = lambda bi, c: (0, 0)
    const3 = lambda bi, c: (0, 0, 0)
    vmem = (4 * _nbytes((tt, w), F32) + 2 * _nbytes((tt, w), BF16) + 3 * _nbytes((tt + SUBLANES, w), F32)
            + 12 * _nbytes((tt, w), F32) + 4 * _nbytes((N_LRU_BLOCKS, LRU_BLOCK, LRU_BLOCK), BF16))
    return pl.pallas_call(
        functools.partial(_lru_kernel, tt=tt, seq_start=seq_start, t_last=t_last),
        grid=(b, t // tt),
        in_specs=[pl.BlockSpec((1, tt, w), seq), pl.BlockSpec((1, tt, w), seq),
                  pl.BlockSpec((1, SUBLANES, w), per_b), pl.BlockSpec((1, 1, w), per_b),
                  pl.BlockSpec((CONV_WIDTH, w), const2), pl.BlockSpec((1, w), const2),
                  pl.BlockSpec((N_LRU_BLOCKS, LRU_BLOCK, LRU_BLOCK), const3), pl.BlockSpec((1, w), const2),
                  pl.BlockSpec((N_LRU_BLOCKS, LRU_BLOCK, LRU_BLOCK), const3), pl.BlockSpec((1, w), const2),
                  pl.BlockSpec((1, w), const2)],
        out_specs=[pl.BlockSpec((1, tt, w), seq), pl.BlockSpec((1, 1, w), per_b),
                   pl.BlockSpec((1, CONV_WIDTH - 1, w), per_b)],
        out_shape=[jax.ShapeDtypeStruct((b, t, w), BF16), jax.ShapeDtypeStruct((b, 1, w), F32),
                   jax.ShapeDtypeStruct((b, CONV_WIDTH - 1, w), F32)],
        scratch_shapes=[pltpu.VMEM((tt + SUBLANES, w), F32), pltpu.VMEM((tt, w), F32), pltpu.VMEM((tt, w), F32),
                        pltpu.VMEM((1, w), F32)],
        compiler_params=_params(("parallel", "arbitrary"), vmem),
        name="conv_rglru",
    )(xr, gr, cbuf, h0, lp["conv_w"], lp["conv_b"], lp["w_rg"], lp["b_rg"], lp["w_ig"], lp["b_ig"], lp["lam"])


def _attn_kernel(q_ref, k_ref, v_ref, km_ref, vm_ref, lq1_ref, lk1_ref, lq2_ref, lk2_ref, g_ref, o_ref,
                 qq_scr, m_scr, l_scr, acc_scr, *, tq):
    qi = pl.program_id(2)
    q = q_ref[0]
    lane = lax.broadcasted_iota(jnp.int32, q.shape, 1)
    zero = jnp.zeros_like(q)
    qq_scr[0:tq, :] = jnp.where(lane < HEAD_DIM, q, zero)
    qq_scr[tq:2 * tq, :] = jnp.where(lane >= HEAD_DIM, q, zero)
    m_scr[...] = jnp.full(m_scr.shape, NEG, F32)
    l_scr[...] = jnp.zeros(l_scr.shape, F32)
    acc_scr[...] = jnp.zeros(acc_scr.shape, F32)

    def update(kb, vb, mask):
        s = lax.dot_general(qq_scr[...], kb, (((1,), (1,)), ((), ())), preferred_element_type=F32)
        if mask is not None:
            s = jnp.where(mask, s, NEG)
        m_old = m_scr[...]
        m_new = jnp.maximum(m_old, jnp.max(s, axis=1, keepdims=True))
        alpha = jnp.exp(m_old - m_new)
        p = jnp.exp(s - m_new)
        l_scr[...] = alpha * l_scr[...] + jnp.sum(p, axis=1, keepdims=True)
        acc_scr[...] = alpha * acc_scr[...] + jnp.dot(p.astype(BF16), vb, preferred_element_type=F32)
        m_scr[...] = m_new

    col_m = lax.broadcasted_iota(jnp.int32, (2 * tq, km_ref.shape[0]), 1)
    update(km_ref[...].astype(BF16), vm_ref[...].astype(BF16), col_m < N_META)

    def full_tile(j, carry):
        rows = pl.ds(pl.multiple_of(j * tq, tq), tq)
        update(k_ref[0, rows, :].astype(BF16), v_ref[0, rows, :].astype(BF16), None)
        return carry

    lax.fori_loop(0, qi, full_tile, 0)

    rows = pl.ds(pl.multiple_of(qi * tq, tq), tq)
    row = lax.broadcasted_iota(jnp.int32, (2 * tq, tq), 0)
    col = lax.broadcasted_iota(jnp.int32, (2 * tq, tq), 1)
    update(k_ref[0, rows, :].astype(BF16), v_ref[0, rows, :].astype(BF16),
           col <= jnp.where(row >= tq, row - tq, row))

    lam = _diff_lambda(lq1_ref, lk1_ref, lq2_ref, lk2_ref)
    oa = acc_scr[...] / l_scr[...]
    o = oa[0:tq] - lam * oa[tq:2 * tq]
    o_ref[0] = (_rms(o, g_ref[...]) * (1.0 - LAMBDA_INIT)).astype(BF16)


def _attn_prompt(q, k, v, k_meta, v_meta, ap, *, tq):
    b, t, _ = q.shape
    assert t % tq == 0
    hd = lambda bi, h, i: (bi, i, h)
    kv = lambda bi, h, i: (bi, 0, h)
    meta = lambda bi, h, i: (0, h)
    const = lambda bi, h, i: (0, 0)
    vmem = (4 * _nbytes((t, V_DIM), F32) + 4 * _nbytes((LANES, V_DIM), F32) + 4 * _nbytes((tq, V_DIM), BF16)
            + _nbytes((2 * tq, V_DIM), BF16) + 3 * _nbytes((2 * tq, V_DIM), F32) + 6 * _nbytes((2 * tq, tq), F32))
    return pl.pallas_call(
        functools.partial(_attn_kernel, tq=tq),
        grid=(b, N_HEADS, t // tq),
        in_specs=[pl.BlockSpec((1, tq, V_DIM), hd), pl.BlockSpec((1, t, V_DIM), kv), pl.BlockSpec((1, t, V_DIM), kv),
                  pl.BlockSpec((LANES, V_DIM), meta), pl.BlockSpec((LANES, V_DIM), meta),
                  pl.BlockSpec((1, HEAD_DIM), const), pl.BlockSpec((1, HEAD_DIM), const),
                  pl.BlockSpec((1, HEAD_DIM), const), pl.BlockSpec((1, HEAD_DIM), const),
                  pl.BlockSpec((1, V_DIM), const)],
        out_specs=pl.BlockSpec((1, tq, V_DIM), hd),
        out_shape=jax.ShapeDtypeStruct((b, t, V_WIDTH), BF16),
        scratch_shapes=[pltpu.VMEM((2 * tq, V_DIM), BF16), pltpu.VMEM((2 * tq, 1), F32), pltpu.VMEM((2 * tq, 1), F32),
                        pltpu.VMEM((2 * tq, V_DIM), F32)],
        compiler_params=_params(("parallel", "parallel", "arbitrary"), vmem),
        name="attn_prompt",
    )(q, k, v, k_meta, v_meta, ap["lam_q1"], ap["lam_k1"], ap["lam_q2"], ap["lam_k2"], ap["g_subln"])


def _decode_kernel(pt_ref, qbd_ref, kn_ref, vn_ref, *rest, n_pg, dec_seq):
    del pt_ref
    k_refs, v_refs = rest[:n_pg], rest[n_pg:2 * n_pg]
    (lq1_ref, lk1_ref, lq2_ref, lk2_ref, g_ref, o_ref, kb_scr, vb_scr, m_scr, l_scr, acc_scr) = rest[2 * n_pg:]
    s_idx = pl.program_id(1)
    eye = (lax.broadcasted_iota(jnp.int32, (LANES, LANES), 0) == lax.broadcasted_iota(jnp.int32, (LANES, LANES), 1))

    def to_col(row_vec):
        return jnp.sum(jnp.where(eye, row_vec, 0.0), axis=1, keepdims=True)

    def update(kb, vb, mask):
        st = jnp.dot(kb, qbd_ref[0], preferred_element_type=F32)
        if mask is not None:
            st = jnp.where(mask, st, NEG)
        m_old = m_scr[...]
        m_new = jnp.maximum(m_old, jnp.max(st, axis=0, keepdims=True))
        alpha = jnp.exp(m_old - m_new)
        p = jnp.exp(st - m_new)
        l_scr[...] = alpha * l_scr[...] + jnp.sum(p, axis=0, keepdims=True)
        m_scr[...] = m_new
        acc_scr[...] = to_col(alpha) * acc_scr[...] + jnp.dot(p.T.astype(BF16), vb, preferred_element_type=F32)

    @pl.when(s_idx == 0)
    def _():
        m_scr[...] = jnp.full(m_scr.shape, NEG, F32)
        l_scr[...] = jnp.zeros(l_scr.shape, F32)
        acc_scr[...] = jnp.zeros(acc_scr.shape, F32)
        key = lax.broadcasted_iota(jnp.int32, (PAGE_SIZE, LANES), 0)
        qry = lax.broadcasted_iota(jnp.int32, (PAGE_SIZE, LANES), 1) & (dec_seq - 1)
        update(kn_ref[0], vn_ref[0], key <= qry)

    for p in range(n_pg):
        kb_scr[p * PAGE_SIZE:(p + 1) * PAGE_SIZE, :] = k_refs[p][0].astype(BF16)
        vb_scr[p * PAGE_SIZE:(p + 1) * PAGE_SIZE, :] = v_refs[p][0].astype(BF16)
    update(kb_scr[...], vb_scr[...], None)

    @pl.when(s_idx == pl.num_programs(1) - 1)
    def _():
        lam = _diff_lambda(lq1_ref, lk1_ref, lq2_ref, lk2_ref)
        l_col = to_col(l_scr[...])
        rows_per_head = 2 * dec_seq
        for h in range(N_HEADS):
            r0 = h * rows_per_head
            blk = acc_scr[r0:r0 + rows_per_head, h * V_DIM:(h + 1) * V_DIM] / l_col[r0:r0 + rows_per_head]
            d = blk - lam * pltpu.roll(blk, dec_seq, axis=0)
            o_ref[0, :, h * V_DIM:(h + 1) * V_DIM] = _rms(d, g_ref[...]) * (1.0 - LAMBDA_INIT)


def _attn_sample(qbd, k_new, v_new, cache_k, cache_v, page_table, ap, *, n_pg, dec_seq):
    b, n_pages = page_table.shape
    assert n_pages % n_pg == 0 and 2 * dec_seq == SUBLANES and dec_seq & (dec_seq - 1) == 0
    per_b = lambda bi, s, pt: (bi, 0, 0)
    const = lambda bi, s, pt: (0, 0)
    page = lambda p: (lambda bi, s, pt: (pt[bi, s * n_pg + p], 0, 0))
    rows = n_pg * PAGE_SIZE
    vmem = (4 * n_pg * _nbytes((PAGE_SIZE, V_WIDTH), F32) + 2 * _nbytes((rows, V_WIDTH), BF16)
            + 6 * _nbytes((PAGE_SIZE, V_WIDTH), BF16) + 3 * _nbytes((LANES, V_WIDTH), F32)
            + 6 * _nbytes((rows, LANES), F32))
    grid_spec = pltpu.PrefetchScalarGridSpec(
        num_scalar_prefetch=1,
        grid=(b, n_pages // n_pg),
        in_specs=([pl.BlockSpec((1, ATTN_WIDTH, LANES), per_b), pl.BlockSpec((1, PAGE_SIZE, ATTN_WIDTH), per_b),
                   pl.BlockSpec((1, PAGE_SIZE, V_WIDTH), per_b)]
                  + [pl.BlockSpec((1, PAGE_SIZE, ATTN_WIDTH), page(p)) for p in range(n_pg)]
                  + [pl.BlockSpec((1, PAGE_SIZE, V_WIDTH), page(p)) for p in range(n_pg)]
                  + [pl.BlockSpec((1, HEAD_DIM), const)] * 4 + [pl.BlockSpec((1, V_DIM), const)]),
        out_specs=pl.BlockSpec((1, SUBLANES, V_WIDTH), per_b),
        scratch_shapes=[pltpu.VMEM((rows, ATTN_WIDTH), BF16), pltpu.VMEM((rows, V_WIDTH), BF16),
                        pltpu.VMEM((1, LANES), F32), pltpu.VMEM((1, LANES), F32), pltpu.VMEM((LANES, V_WIDTH), F32)],
    )
    return pl.pallas_call(
        functools.partial(_decode_kernel, n_pg=n_pg, dec_seq=dec_seq),
        grid_spec=grid_spec,
        out_shape=jax.ShapeDtypeStruct((b, SUBLANES, V_WIDTH), F32),
        compiler_params=_params(("parallel", "arbitrary"), vmem),
        name="attn_sample",
    )(page_table, qbd, k_new, v_new, *([cache_k] * n_pg), *([cache_v] * n_pg),
      ap["lam_q1"], ap["lam_k1"], ap["lam_q2"], ap["lam_k2"], ap["g_subln"])


def _merge_kernel(o_ref, hg_ref, ga_ref, gl_ref, x_ref, wa_ref, wl_ref, wo_ref, g_ref, out_ref):
    ya = jnp.dot(o_ref[...], wa_ref[...], preferred_element_type=F32)
    yl = jnp.dot(hg_ref[...], wl_ref[...], preferred_element_type=F32)
    merged = jax.nn.sigmoid(ga_ref[...]) * ya + jax.nn.sigmoid(gl_ref[...]) * yl
    mo = jnp.dot(merged.astype(BF16), wo_ref[...], preferred_element_type=F32)
    out_ref[...] = x_ref[...] + _rms(mo, g_ref[...])


def _merge(o, hg, gates, x, wa, wl, wo, g, *, tm):
    m_rows = x.shape[0]
    assert m_rows % tm == 0
    row = lambda m: (m, 0)
    const = lambda m: (0, 0)
    vmem = (2 * (2 * _nbytes((tm, V_WIDTH), BF16) + 4 * _nbytes((tm, D_MODEL), F32))
            + 2 * (2 * _nbytes((V_WIDTH, D_MODEL), BF16) + _nbytes((D_MODEL, D_MODEL), BF16))
            + 6 * _nbytes((tm, D_MODEL), F32))
    return pl.pallas_call(
        _merge_kernel,
        grid=(m_rows // tm,),
        in_specs=[pl.BlockSpec((tm, V_WIDTH), row), pl.BlockSpec((tm, LRU_WIDTH), row),
                  pl.BlockSpec((tm, D_MODEL), row), pl.BlockSpec((tm, D_MODEL), lambda m: (m, 1)),
                  pl.BlockSpec((tm, D_MODEL), row),
                  pl.BlockSpec((V_WIDTH, D_MODEL), const), pl.BlockSpec((LRU_WIDTH, D_MODEL), const),
                  pl.BlockSpec((D_MODEL, D_MODEL), const), pl.BlockSpec((1, D_MODEL), const)],
        out_specs=pl.BlockSpec((tm, D_MODEL), row),
        out_shape=jax.ShapeDtypeStruct((m_rows, D_MODEL), F32),
        compiler_params=_params(("parallel",), vmem),
        name="merge_out_proj",
    )(o, hg, gates, gates, x, wa, wl, wo, g)


def _mlp_kernel(x_ref, gpre_ref, wup_ref, wdn_ref, gpost_ref, out_ref, hn_scr, acc_scr):
    f = pl.program_id(1)

    @pl.when(f == 0)
    def _():
        hn_scr[...] = _rms(x_ref[...], gpre_ref[...]).astype(BF16)
        acc_scr[...] = jnp.zeros(acc_scr.shape, F32)

    u = jnp.square(jnp.maximum(jnp.dot(hn_scr[...], wup_ref[...], preferred_element_type=F32), 0.0))
    acc_scr[...] += jnp.dot(u.astype(BF16), wdn_ref[...], preferred_element_type=F32)

    @pl.when(f == pl.num_programs(1) - 1)
    def _():
        out_ref[...] = x_ref[...] + _rms(acc_scr[...], gpost_ref[...])


def _mlp(x, gpre, wup, wdn, gpost, *, tm, tf):
    m_rows = x.shape[0]
    assert m_rows % tm == 0 and D_FF % tf == 0
    vmem = (4 * _nbytes((tm, D_MODEL), F32) + _nbytes((tm, D_MODEL), BF16) + _nbytes((tm, D_MODEL), F32)
            + 4 * _nbytes((D_MODEL, tf), BF16) + 3 * _nbytes((tm, tf), F32) + 2 * _nbytes((tm, D_MODEL), F32))
    return pl.pallas_call(
        _mlp_kernel,
        grid=(m_rows // tm, D_FF // tf),
        in_specs=[pl.BlockSpec((tm, D_MODEL), lambda m, f: (m, 0)), pl.BlockSpec((1, D_MODEL), lambda m, f: (0, 0)),
                  pl.BlockSpec((D_MODEL, tf), lambda m, f: (0, f)), pl.BlockSpec((tf, D_MODEL), lambda m, f: (f, 0)),
                  pl.BlockSpec((1, D_MODEL), lambda m, f: (0, 0))],
        out_specs=pl.BlockSpec((tm, D_MODEL), lambda m, f: (m, 0)),
        out_shape=jax.ShapeDtypeStruct((m_rows, D_MODEL), F32),
        scratch_shapes=[pltpu.VMEM((tm, D_MODEL), BF16), pltpu.VMEM((tm, D_MODEL), F32)],
        compiler_params=_params(("parallel", "arbitrary"), vmem),
        name="mlp",
    )(x, gpre, wup, wdn, gpost)


_PROMPT_TILES = dict(in_tm=1024, in_tn=512, lru_tt=512, attn_tq=256, merge_tm=256, mlp_tm=512, mlp_tf=1024)
_DECODE_PAGES_PER_STEP = 8


def _block_diag_queries(q, dec_seq):
    b = q.shape[0]
    qt = q.reshape(b, dec_seq, N_HEADS * 2, HEAD_DIM).transpose(0, 2, 3, 1)
    onehot = jnp.eye(N_HEADS * 2, dtype=q.dtype)
    qbd = qt[:, :, :, None, :] * onehot[None, :, None, :, None]
    qbd = qbd.reshape(b, ATTN_WIDTH, N_HEADS * 2 * dec_seq)
    return jnp.pad(qbd, ((0, 0), (0, 0), (0, LANES - N_HEADS * 2 * dec_seq)))


def kernel(x_prompt, x_sample, cache_k, cache_v, state_lru, state_conv, page_table, meta_tokens, g_pre_mix, w_in,
           lam_q1, lam_k1, lam_q2, lam_k2, g_subln, w_attn_br, conv_w, conv_b, w_rg, b_rg, w_ig, b_ig, lru_lambda,
           w_lru_br, w_out, g_post_mix, g_pre_mlp, w_up, w_down, g_post_mlp):
    depth = w_in.shape[0]
    assert depth == 1
    batch, seq, _ = x_prompt.shape
    dec_batch, dec_seq, _ = x_sample.shape
    n_phys = cache_k.shape[1]
    tl = _PROMPT_TILES

    w_in_b = w_in[0].astype(BF16)
    w_attn_b = w_attn_br[0].astype(BF16)
    w_lru_b = w_lru_br[0].astype(BF16)
    w_out_b = w_out[0].astype(BF16)
    w_up_b = w_up[0].astype(BF16)
    w_down_b = w_down[0].astype(BF16)
    lp = dict(conv_w=conv_w[0], conv_b=conv_b, w_rg=w_rg[0].astype(BF16), b_rg=b_rg, w_ig=w_ig[0].astype(BF16),
              b_ig=b_ig, lam=lru_lambda)
    ap = dict(lam_q1=lam_q1, lam_k1=lam_k1, lam_q2=lam_q2, lam_k2=lam_k2, g_subln=g_subln)

    n_small = N_META + dec_batch * dec_seq
    x_small = jnp.concatenate([meta_tokens, x_sample.reshape(dec_batch * dec_seq, D_MODEL)], axis=0)
    q_sm, k_sm, v_sm, xr_sm, gr_sm, gates_sm = _in_proj(x_small, g_pre_mix, w_in_b, tm=n_small, tn=tl["in_tn"])

    zeros8 = jnp.zeros((1, SUBLANES, LRU_WIDTH), F32)
    _, h_meta, conv_meta = _lru(xr_sm[None, :N_META], gr_sm[None, :N_META], zeros8, jnp.zeros((1, 1, LRU_WIDTH), F32),
                                lp, tt=N_META, seq_start=True, t_valid=N_META)
    k_meta, v_meta = k_sm[:N_META], v_sm[:N_META]

    xp = x_prompt.reshape(batch * seq, D_MODEL)
    q_p, k_p, v_p, xr_p, gr_p, gates_p = _in_proj(xp, g_pre_mix, w_in_b, tm=tl["in_tm"], tn=tl["in_tn"])
    cbuf_p = jnp.broadcast_to(jnp.pad(conv_meta, ((0, 0), (SUBLANES - CONV_WIDTH + 1, 0), (0, 0))),
                              (batch, SUBLANES, LRU_WIDTH))
    hg_p, lru_prompt, conv_prompt = _lru(xr_p.reshape(batch, seq, LRU_WIDTH), gr_p.reshape(batch, seq, LRU_WIDTH),
                                         cbuf_p, jnp.broadcast_to(h_meta, (batch, 1, LRU_WIDTH)), lp,
                                         tt=tl["lru_tt"], seq_start=False, t_valid=seq)
    pad_meta = ((0, LANES - N_META), (0, 0))
    o_p = _attn_prompt(q_p.reshape(batch, seq, ATTN_WIDTH), k_p.reshape(batch, seq, ATTN_WIDTH),
                       v_p.reshape(batch, seq, V_WIDTH), jnp.pad(k_meta, pad_meta), jnp.pad(v_meta, pad_meta), ap,
                       tq=tl["attn_tq"])
    x1_p = _merge(o_p.reshape(batch * seq, V_WIDTH), hg_p.reshape(batch * seq, LRU_WIDTH), gates_p, xp,
                  w_attn_b, w_lru_b, w_out_b, g_post_mix, tm=tl["merge_tm"])
    y_prompt = _mlp(x1_p, g_pre_mlp, w_up_b, w_down_b, g_post_mlp, tm=tl["mlp_tm"], tf=tl["mlp_tf"])

    n_s = dec_batch * dec_seq
    xs = x_sample.reshape(n_s, D_MODEL)
    q_s, k_s, v_s = q_sm[N_META:], k_sm[N_META:], v_sm[N_META:]
    pad_t = ((0, 0), (0, SUBLANES - dec_seq), (0, 0))
    cbuf_s = jnp.pad(state_conv[0], ((0, 0), (SUBLANES - CONV_WIDTH + 1, 0), (0, 0)))
    hg_s, lru_sample, conv_sample = _lru(jnp.pad(xr_sm[N_META:].reshape(dec_batch, dec_seq, LRU_WIDTH), pad_t),
                                         jnp.pad(gr_sm[N_META:].reshape(dec_batch, dec_seq, LRU_WIDTH), pad_t),
                                         cbuf_s, state_lru[0][:, None, :], lp,
                                         tt=SUBLANES, seq_start=False, t_valid=dec_seq)
    pad_pg = ((0, 0), (0, PAGE_SIZE - dec_seq), (0, 0))
    o_s = _attn_sample(_block_diag_queries(q_s.reshape(dec_batch, dec_seq, ATTN_WIDTH), dec_seq),
                       jnp.pad(k_s.astype(BF16).reshape(dec_batch, dec_seq, ATTN_WIDTH), pad_pg),
                       jnp.pad(v_s.astype(BF16).reshape(dec_batch, dec_seq, V_WIDTH), pad_pg),
                       cache_k.reshape(n_phys, PAGE_SIZE, ATTN_WIDTH), cache_v.reshape(n_phys, PAGE_SIZE, V_WIDTH),
                       page_table, ap, n_pg=_DECODE_PAGES_PER_STEP, dec_seq=dec_seq)
    o_s = o_s[:, :dec_seq].reshape(n_s, V_WIDTH).astype(BF16)
    x1_s = _merge(o_s, hg_s[:, :dec_seq].reshape(n_s, LRU_WIDTH), gates_sm[N_META:], xs,
                  w_attn_b, w_lru_b, w_out_b, g_post_mix, tm=n_s)
    y_sample = _mlp(x1_s, g_pre_mlp, w_up_b, w_down_b, g_post_mlp, tm=n_s, tf=tl["mlp_tf"])

    def with_meta(meta_rows, main, width):
        full = jnp.concatenate([jnp.broadcast_to(meta_rows[None], (batch, N_META, width)),
                                main.reshape(batch, seq, width)], axis=1)
        return full.reshape(1, batch, N_META + seq, N_HEADS, width // N_HEADS)

    return (y_prompt.reshape(batch, seq, D_MODEL),
            y_sample.reshape(dec_batch, dec_seq, D_MODEL),
            with_meta(k_meta, k_p, ATTN_WIDTH),
            with_meta(v_meta, v_p, V_WIDTH),
            lru_prompt.reshape(1, batch, LRU_WIDTH),
            conv_prompt.reshape(1, batch, CONV_WIDTH - 1, LRU_WIDTH),
            k_s.reshape(1, dec_batch, dec_seq, N_HEADS, 2 * HEAD_DIM),
            v_s.reshape(1, dec_batch, dec_seq, N_HEADS, V_DIM),
            lru_sample.reshape(1, dec_batch, LRU_WIDTH),
            conv_sample.reshape(1, dec_batch, CONV_WIDTH - 1, LRU_WIDTH))
```

```python
import functools
import math

import jax
import jax.numpy as jnp
from jax import lax
from jax.experimental import pallas as pl
from jax.experimental.pallas import tpu as pltpu

F32 = jnp.float32
BF16 = jnp.bfloat16

D_MODEL = 2048
N_META = 16
N_HEADS = 8
HEAD_DIM = 64
V_DIM = 2 * HEAD_DIM
ATTN_WIDTH = N_HEADS * 2 * HEAD_DIM
V_WIDTH = N_HEADS * V_DIM
LRU_WIDTH = D_MODEL // 2
N_LRU_BLOCKS = 8
LRU_BLOCK = LRU_WIDTH // N_LRU_BLOCKS
CONV_WIDTH = 4
LRU_C = 8.0
D_FF = 4 * D_MODEL
PAGE_SIZE = 128
NEG = -1e30
EPS = 1e-6
LAMBDA_INIT = 0.8 - 0.6 * math.exp(-0.3 * 0)
IN_WIDTH = 2 * ATTN_WIDTH + V_WIDTH + 2 * LRU_WIDTH + 2 * D_MODEL

SUBLANES = 8
LANES = 128
V7X_VMEM_CAP_BYTES = 60 * 1024 * 1024

_IN_SEGMENTS = ((ATTN_WIDTH, BF16), (ATTN_WIDTH, F32), (V_WIDTH, F32), (LRU_WIDTH, F32), (LRU_WIDTH, F32),
                (2 * D_MODEL, F32))


def _nbytes(shape, dtype):
    return math.prod(shape) * jnp.dtype(dtype).itemsize


def _params(semantics, vmem_bytes):
    return pltpu.CompilerParams(dimension_semantics=semantics,
                                vmem_limit_bytes=min(int(vmem_bytes), V7X_VMEM_CAP_BYTES))


def _rms(x, g):
    return x * lax.rsqrt(jnp.mean(x * x, axis=-1, keepdims=True) + EPS) * g


def _diff_lambda(lq1_ref, lk1_ref, lq2_ref, lk2_ref):
    a = jnp.sum(lq1_ref[...] * lk1_ref[...], axis=1, keepdims=True)
    b = jnp.sum(lq2_ref[...] * lk2_ref[...], axis=1, keepdims=True)
    return jnp.exp(a) - jnp.exp(b) + LAMBDA_INIT


def _in_proj_kernel(x_ref, g_ref, w_ref, *rest, seg_steps):
    out_refs, h_scr = rest[:-1], rest[-1]
    n = pl.program_id(1)

    @pl.when(n == 0)
    def _():
        h_scr[...] = _rms(x_ref[...], g_ref[...]).astype(BF16)

    z = jnp.dot(h_scr[...], w_ref[...], preferred_element_type=F32)
    for idx, (o_ref, (lo, hi)) in enumerate(zip(out_refs, seg_steps)):
        @pl.when((n >= lo) & (n < hi))
        def _(o_ref=o_ref, idx=idx):
            val = z * (HEAD_DIM ** -0.5) if idx == 0 else z
            o_ref[...] = val.astype(o_ref.dtype)


def _in_proj(x, g, w, *, tm, tn):
    m_rows = x.shape[0]
    assert m_rows % tm == 0 and all(wd % tn == 0 for wd, _ in _IN_SEGMENTS)
    seg_steps, out_specs, out_shapes = [], [], []
    lo = 0
    for wd, dt in _IN_SEGMENTS:
        cnt = wd // tn
        seg_steps.append((lo, lo + cnt))
        out_specs.append(pl.BlockSpec((tm, tn), lambda m, n, lo=lo, cnt=cnt: (m, jnp.clip(n - lo, 0, cnt - 1))))
        out_shapes.append(jax.ShapeDtypeStruct((m_rows, wd), dt))
        lo += cnt
    vmem = (2 * _nbytes((tm, D_MODEL), F32) + _nbytes((tm, D_MODEL), BF16) + 2 * _nbytes((D_MODEL, tn), BF16)
            + sum(2 * _nbytes((tm, tn), dt) for _, dt in _IN_SEGMENTS) + 3 * _nbytes((tm, tn), F32)
            + _nbytes((tm, D_MODEL), F32))
    return pl.pallas_call(
        functools.partial(_in_proj_kernel, seg_steps=tuple(seg_steps)),
        grid=(m_rows // tm, IN_WIDTH // tn),
        in_specs=[pl.BlockSpec((tm, D_MODEL), lambda m, n: (m, 0)),
                  pl.BlockSpec((1, D_MODEL), lambda m, n: (0, 0)),
                  pl.BlockSpec((D_MODEL, tn), lambda m, n: (0, n))],
        out_specs=out_specs,
        out_shape=out_shapes,
        scratch_shapes=[pltpu.VMEM((tm, D_MODEL), BF16)],
        compiler_params=_params(("parallel", "arbitrary"), vmem),
        name="in_proj",
    )(x, g, w)


def _lru_kernel(xr_ref, gr_ref, cbuf_ref, h0_ref, cw_ref, cb_ref, wrg_ref, brg_ref, wig_ref, big_ref, lam_ref,
                hg_ref, hlast_ref, cnew_ref, ext_scr, a_scr, u_scr, h_scr, *, tt, seq_start, t_last):
    c = pl.program_id(1)

    @pl.when(c == 0)
    def _():
        ext_scr[0:SUBLANES, :] = cbuf_ref[0]
        h_scr[...] = h0_ref[0]

    xr = xr_ref[0]
    ext_scr[SUBLANES:SUBLANES + tt, :] = xr
    cw = cw_ref[...]
    xc = (cb_ref[...] + ext_scr[5:5 + tt, :] * cw[0:1] + ext_scr[6:6 + tt, :] * cw[1:2]
          + ext_scr[7:7 + tt, :] * cw[2:3] + xr * cw[3:4])
    ext_scr[0:SUBLANES, :] = ext_scr[tt:tt + SUBLANES, :]

    xcb = xc.astype(BF16)

    def gate(w_ref, b_ref):
        parts = [jnp.dot(xcb[:, n * LRU_BLOCK:(n + 1) * LRU_BLOCK], w_ref[n], preferred_element_type=F32)
                 for n in range(N_LRU_BLOCKS)]
        return jax.nn.sigmoid(jnp.concatenate(parts, axis=1) + b_ref[...])

    r = gate(wrg_ref, brg_ref)
    i = gate(wig_ref, big_ref)
    lam = lam_ref[...]
    softplus_neg_lam = jnp.maximum(-lam, 0.0) + jnp.log1p(jnp.exp(-jnp.abs(lam)))
    log_a = (-LRU_C) * r * softplus_neg_lam
    a = jnp.exp(log_a)
    mult = jnp.sqrt(-jnp.tanh(log_a) * (a * a + 1.0))
    row = lax.broadcasted_iota(jnp.int32, (tt, LRU_WIDTH), 0)
    if seq_start:
        mult = jnp.where(row + c * tt == 0, 1.0, mult)
    u = mult * i * xc

    rmod = row & (SUBLANES - 1)
    for s in (1, 2, 4):
        ok = rmod >= s
        a_prev = pltpu.roll(a, s, axis=0)
        u_prev = pltpu.roll(u, s, axis=0)
        u = jnp.where(ok, a * u_prev + u, u)
        a = jnp.where(ok, a * a_prev, a)
    a_scr[...] = a
    u_scr[...] = u

    def carry(gidx, h):
        rows = pl.ds(pl.multiple_of(gidx * SUBLANES, SUBLANES), SUBLANES)
        hrows = a_scr[rows, :] * h + u_scr[rows, :]
        u_scr[rows, :] = hrows
        return hrows[SUBLANES - 1:SUBLANES, :]

    h_scr[...] = lax.fori_loop(0, tt // SUBLANES, carry, h_scr[...])
    hg_ref[0] = (u_scr[...] * jax.nn.gelu(gr_ref[0])).astype(BF16)

    @pl.when(c == pl.num_programs(1) - 1)
    def _():
        hlast_ref[0] = u_scr[t_last:t_last + 1, :]
        cnew_ref[0] = ext_scr[SUBLANES + t_last - 2:SUBLANES + t_last + 1, :]


def _lru(xr, gr, cbuf, h0, lp, *, tt, seq_start, t_valid):
    b, t, w = xr.shape
    assert t % tt == 0 and tt % SUBLANES == 0 and (t_valid - 1) // tt == t // tt - 1 and t_valid >= CONV_WIDTH - 1
    t_last = (t_valid - 1) % tt
    assert t_last >= CONV_WIDTH - 2
    seq = lambda bi, c: (bi, c, 0)
    per_b = lambda bi, c: (bi, 0, 0)
    const2 = lambda bi, c: (0, 0)
    const3 = lambda bi, c: (0, 0, 0)
    vmem = (4 * _nbytes((tt, w), F32) + 2 * _nbytes((tt, w), BF16) + 3 * _nbytes((tt + SUBLANES, w), F32)
            + 12 * _nbytes((tt, w), F32) + 4 * _nbytes((N_LRU_BLOCKS, LRU_BLOCK, LRU_BLOCK), BF16))
    return pl.pallas_call(
        functools.partial(_lru_kernel, tt=tt, seq_start=seq_start, t_last=t_last),
        grid=(b, t // tt),
        in_specs=[pl.BlockSpec((1, tt, w), seq), pl.BlockSpec((1, tt, w), seq),
                  pl.BlockSpec((1, SUBLANES, w), per_b), pl.BlockSpec((1, 1, w), per_b),
                  pl.BlockSpec((CONV_WIDTH, w), const2), pl.BlockSpec((1, w), const2),
                  pl.BlockSpec((N_LRU_BLOCKS, LRU_BLOCK, LRU_BLOCK), const3), pl.BlockSpec((1, w), const2),
                  pl.BlockSpec((N_LRU_BLOCKS, LRU_BLOCK, LRU_BLOCK), const3), pl.BlockSpec((1, w), const2),
                  pl.BlockSpec((1, w), const2)],
        out_specs=[pl.BlockSpec((1, tt, w), seq), pl.BlockSpec((1, 1, w), per_b),
                   pl.BlockSpec((1, CONV_WIDTH - 1, w), per_b)],
        out_shape=[jax.ShapeDtypeStruct((b, t, w), BF16), jax.ShapeDtypeStruct((b, 1, w), F32),
                   jax.ShapeDtypeStruct((b, CONV_WIDTH - 1, w), F32)],
        scratch_shapes=[pltpu.VMEM((tt + SUBLANES, w), F32), pltpu.VMEM((tt, w), F32), pltpu.VMEM((tt, w), F32),
                        pltpu.VMEM((1, w), F32)],
        compiler_params=_params(("parallel", "arbitrary"), vmem),
        name="conv_rglru",
    )(xr, gr, cbuf, h0, lp["conv_w"], lp["conv_b"], lp["w_rg"], lp["b_rg"], lp["w_ig"], lp["b_ig"], lp["lam"])


def _attn_kernel(q_ref, k_ref, v_ref, km_ref, vm_ref, lq1_ref, lk1_ref, lq2_ref, lk2_ref, g_ref, o_ref,
                 kb_scr, vt_scr, *, tq, tk):
    t = k_ref.shape[1]
    kb_scr[...] = k_ref[0].astype(BF16)
    vt_scr[...] = v_ref[0].T.astype(BF16)
    kmb = km_ref[...].astype(BF16)
    vmt = vm_ref[...].T.astype(BF16)
    lam = _diff_lambda(lq1_ref, lk1_ref, lq2_ref, lk2_ref)
    g = g_ref[...]
    lane = lax.broadcasted_iota(jnp.int32, (tq, V_DIM), 1)

    def update(state, qqt, kc, vtc, mask):
        m_old, l_old, acc = state
        st = jnp.dot(kc, qqt, preferred_element_type=F32)
        if mask is not None:
            st = jnp.where(mask, st, NEG)
        m_new = jnp.maximum(m_old, jnp.max(st, axis=0, keepdims=True))
        alpha = jnp.exp(m_old - m_new)
        p = jnp.exp(st - m_new)
        l_new = alpha * l_old + jnp.sum(p, axis=0, keepdims=True)
        acc = alpha * acc + jnp.dot(vtc, p.astype(BF16), preferred_element_type=F32)
        return m_new, l_new, acc

    for qi in range(t // tq):
        q = q_ref[0, qi * tq:(qi + 1) * tq, :].astype(F32)
        qq = jnp.concatenate([jnp.where(lane < HEAD_DIM, q, 0.0), jnp.where(lane >= HEAD_DIM, q, 0.0)], axis=0)
        qqt = qq.T.astype(BF16)
        state = (jnp.full((1, 2 * tq), NEG, F32), jnp.zeros((1, 2 * tq), F32), jnp.zeros((V_DIM, 2 * tq), F32))
        key_m = lax.broadcasted_iota(jnp.int32, (km_ref.shape[0], 2 * tq), 0)
        state = update(state, qqt, kmb, vmt, key_m < N_META)
        n_keys = (qi + 1) * tq
        for ks in range(0, n_keys, tk):
            w = min(tk, n_keys - ks)
            mask = None
            if ks + w > qi * tq:
                key = ks + lax.broadcasted_iota(jnp.int32, (w, 2 * tq), 0)
                qpos = qi * tq + (lax.broadcasted_iota(jnp.int32, (w, 2 * tq), 1) & (tq - 1))
                mask = key <= qpos
            state = update(state, qqt, kb_scr[ks:ks + w, :], vt_scr[:, ks:ks + w], mask)
        _, l_fin, acc = state
        oa = acc / l_fin
        ot = oa[:, 0:tq] - lam * oa[:, tq:2 * tq]
        o_ref[0, qi * tq:(qi + 1) * tq, :] = (_rms(ot.T, g) * (1.0 - LAMBDA_INIT)).astype(BF16)


def _attn_prompt(q, k, v, k_meta, v_meta, ap, *, tq, tk):
    b, t, _ = q.shape
    assert t % tq == 0 and tk % tq == 0 and tq & (tq - 1) == 0
    hd = lambda bi, h: (bi, 0, h)
    meta = lambda bi, h: (0, h)
    const = lambda bi, h: (0, 0)
    vmem = (4 * _nbytes((t, V_DIM), F32) + 4 * _nbytes((t, V_DIM), BF16) + 4 * _nbytes((LANES, V_DIM), F32)
            + 2 * _nbytes((t, V_DIM), BF16) + 2 * _nbytes((t, V_DIM), F32) + 8 * _nbytes((tk, 2 * tq), F32))
    return pl.pallas_call(
        functools.partial(_attn_kernel, tq=tq, tk=tk),
        grid=(b, N_HEADS),
        in_specs=[pl.BlockSpec((1, t, V_DIM), hd), pl.BlockSpec((1, t, V_DIM), hd), pl.BlockSpec((1, t, V_DIM), hd),
                  pl.BlockSpec((LANES, V_DIM), meta), pl.BlockSpec((LANES, V_DIM), meta),
                  pl.BlockSpec((1, HEAD_DIM), const), pl.BlockSpec((1, HEAD_DIM), const),
                  pl.BlockSpec((1, HEAD_DIM), const), pl.BlockSpec((1, HEAD_DIM), const),
                  pl.BlockSpec((1, V_DIM), const)],
        out_specs=pl.BlockSpec((1, t, V_DIM), hd),
        out_shape=jax.ShapeDtypeStruct((b, t, V_WIDTH), BF16),
        scratch_shapes=[pltpu.VMEM((t, V_DIM), BF16), pltpu.VMEM((V_DIM, t), BF16)],
        compiler_params=_params(("parallel", "parallel"), vmem),
        name="attn_prompt",
    )(q, k, v, k_meta, v_meta, ap["lam_q1"], ap["lam_k1"], ap["lam_q2"], ap["lam_k2"], ap["g_subln"])


def _decode_kernel(pt_ref, qm_ref, kn_ref, vn_ref, *rest, n_pg, dec_seq):
    del pt_ref
    k_refs, v_refs = rest[:n_pg], rest[n_pg:2 * n_pg]
    (lq1_ref, lk1_ref, lq2_ref, lk2_ref, g_ref, o_ref, kb_scr, vb_scr, bias_scr, m_scr, l_scr, acc_scr) = rest[2 * n_pg:]
    s_idx = pl.program_id(1)
    n_rows = 2 * N_HEADS * dec_seq
    page_rows = PAGE_SIZE * N_HEADS

    def update(kb, vb, bias):
        s = lax.dot_general(qm_ref[0], kb, (((1,), (1,)), ((), ())), preferred_element_type=F32) + bias
        m_old = m_scr[...]
        m_new = jnp.maximum(m_old, jnp.max(s, axis=1, keepdims=True))
        alpha = jnp.exp(m_old - m_new)
        p = jnp.exp(s - m_new)
        l_scr[...] = alpha * l_scr[...] + jnp.sum(p, axis=1, keepdims=True)
        m_scr[...] = m_new
        acc_scr[...] = alpha * acc_scr[...] + jnp.dot(p.astype(BF16), vb, preferred_element_type=F32)

    def head_bias(width, extra=None):
        row = lax.broadcasted_iota(jnp.int32, (n_rows, width), 0)
        col = lax.broadcasted_iota(jnp.int32, (n_rows, width), 1)
        ok = (col & (N_HEADS - 1)) == row // (2 * dec_seq)
        if extra is not None:
            ok = ok & extra(row, col)
        return jnp.where(ok, 0.0, NEG)

    @pl.when(s_idx == 0)
    def _():
        m_scr[...] = jnp.full(m_scr.shape, NEG, F32)
        l_scr[...] = jnp.zeros(l_scr.shape, F32)
        acc_scr[...] = jnp.zeros(acc_scr.shape, F32)
        bias_scr[...] = head_bias(n_pg * page_rows)
        causal = lambda row, col: col // N_HEADS <= (row & (dec_seq - 1))
        update(kn_ref[0], vn_ref[0], head_bias(kn_ref.shape[1], causal))

    for p in range(n_pg):
        kb_scr[p * page_rows:(p + 1) * page_rows, :] = k_refs[p][0, 0].reshape(page_rows, V_DIM).astype(BF16)
        vb_scr[p * page_rows:(p + 1) * page_rows, :] = v_refs[p][0, 0].reshape(page_rows, V_DIM).astype(BF16)
    update(kb_scr[...], vb_scr[...], bias_scr[...])

    @pl.when(s_idx == pl.num_programs(1) - 1)
    def _():
        lam = _diff_lambda(lq1_ref, lk1_ref, lq2_ref, lk2_ref)
        oa = acc_scr[...] / l_scr[...]
        d = oa - lam * pltpu.roll(oa, n_rows - dec_seq, axis=0)
        o_ref[0] = _rms(d, g_ref[...]) * (1.0 - LAMBDA_INIT)


def _attn_sample(qm, k_new, v_new, cache_k, cache_v, page_table, ap, *, n_pg, dec_seq):
    b, n_pages = page_table.shape
    assert n_pages % n_pg == 0 and dec_seq & (dec_seq - 1) == 0 and N_HEADS & (N_HEADS - 1) == 0
    n_rows = 2 * N_HEADS * dec_seq
    page_rows = PAGE_SIZE * N_HEADS
    per_b = lambda bi, s, pt: (bi, 0, 0)
    const = lambda bi, s, pt: (0, 0)
    page = lambda p: (lambda bi, s, pt: (0, pt[bi, s * n_pg + p], 0, 0, 0))
    rows = n_pg * page_rows
    vmem = (4 * n_pg * _nbytes((page_rows, V_DIM), F32) + 2 * _nbytes((rows, V_DIM), BF16)
            + 6 * _nbytes((n_rows, rows), F32) + 2 * _nbytes((rows, V_DIM), F32))
    grid_spec = pltpu.PrefetchScalarGridSpec(
        num_scalar_prefetch=1,
        grid=(b, n_pages // n_pg),
        in_specs=([pl.BlockSpec((1, n_rows, V_DIM), per_b), pl.BlockSpec((1, LANES, V_DIM), per_b),
                   pl.BlockSpec((1, LANES, V_DIM), per_b)]
                  + [pl.BlockSpec((1, 1, PAGE_SIZE, N_HEADS, V_DIM), page(p)) for p in range(n_pg)]
                  + [pl.BlockSpec((1, 1, PAGE_SIZE, N_HEADS, V_DIM), page(p)) for p in range(n_pg)]
                  + [pl.BlockSpec((1, HEAD_DIM), const)] * 4 + [pl.BlockSpec((1, V_DIM), const)]),
        out_specs=pl.BlockSpec((1, n_rows, V_DIM), per_b),
        scratch_shapes=[pltpu.VMEM((rows, V_DIM), BF16), pltpu.VMEM((rows, V_DIM), BF16),
                        pltpu.VMEM((n_rows, rows), F32), pltpu.VMEM((n_rows, 1), F32), pltpu.VMEM((n_rows, 1), F32),
                        pltpu.VMEM((n_rows, V_DIM), F32)],
    )
    return pl.pallas_call(
        functools.partial(_decode_kernel, n_pg=n_pg, dec_seq=dec_seq),
        grid_spec=grid_spec,
        out_shape=jax.ShapeDtypeStruct((b, n_rows, V_DIM), F32),
        compiler_params=_params(("parallel", "arbitrary"), vmem),
        name="attn_sample",
    )(page_table, qm, k_new, v_new, *([cache_k] * n_pg), *([cache_v] * n_pg),
      ap["lam_q1"], ap["lam_k1"], ap["lam_q2"], ap["lam_k2"], ap["g_subln"])


def _merge_kernel(o_ref, hg_ref, ga_ref, gl_ref, x_ref, wa_ref, wl_ref, wo_ref, g_ref, out_ref):
    ya = jnp.dot(o_ref[...], wa_ref[...], preferred_element_type=F32)
    yl = jnp.dot(hg_ref[...], wl_ref[...], preferred_element_type=F32)
    merged = jax.nn.sigmoid(ga_ref[...]) * ya + jax.nn.sigmoid(gl_ref[...]) * yl
    mo = jnp.dot(merged.astype(BF16), wo_ref[...], preferred_element_type=F32)
    out_ref[...] = x_ref[...] + _rms(mo, g_ref[...])


def _merge(o, hg, gates, x, wa, wl, wo, g, *, tm):
    m_rows = x.shape[0]
    assert m_rows % tm == 0
    row = lambda m: (m, 0)
    const = lambda m: (0, 0)
    vmem = (2 * (2 * _nbytes((tm, V_WIDTH), BF16) + 4 * _nbytes((tm, D_MODEL), F32))
            + 2 * (2 * _nbytes((V_WIDTH, D_MODEL), BF16) + _nbytes((D_MODEL, D_MODEL), BF16))
            + 6 * _nbytes((tm, D_MODEL), F32))
    return pl.pallas_call(
        _merge_kernel,
        grid=(m_rows // tm,),
        in_specs=[pl.BlockSpec((tm, V_WIDTH), row), pl.BlockSpec((tm, LRU_WIDTH), row),
                  pl.BlockSpec((tm, D_MODEL), row), pl.BlockSpec((tm, D_MODEL), lambda m: (m, 1)),
                  pl.BlockSpec((tm, D_MODEL), row),
                  pl.BlockSpec((V_WIDTH, D_MODEL), const), pl.BlockSpec((LRU_WIDTH, D_MODEL), const),
                  pl.BlockSpec((D_MODEL, D_MODEL), const), pl.BlockSpec((1, D_MODEL), const)],
        out_specs=pl.BlockSpec((tm, D_MODEL), row),
        out_shape=jax.ShapeDtypeStruct((m_rows, D_MODEL), F32),
        compiler_params=_params(("parallel",), vmem),
        name="merge_out_proj",
    )(o, hg, gates, gates, x, wa, wl, wo, g)


def _mlp_kernel(x_ref, gpre_ref, wup_ref, wdn_ref, gpost_ref, out_ref, hn_scr, acc_scr):
    f = pl.program_id(1)

    @pl.when(f == 0)
    def _():
        hn_scr[...] = _rms(x_ref[...], gpre_ref[...]).astype(BF16)
        acc_scr[...] = jnp.zeros(acc_scr.shape, F32)

    u = jnp.square(jnp.maximum(jnp.dot(hn_scr[...], wup_ref[...], preferred_element_type=F32), 0.0))
    acc_scr[...] += jnp.dot(u.astype(BF16), wdn_ref[...], preferred_element_type=F32)

    @pl.when(f == pl.num_programs(1) - 1)
    def _():
        out_ref[...] = x_ref[...] + _rms(acc_scr[...], gpost_ref[...])


def _mlp(x, gpre, wup, wdn, gpost, *, tm, tf):
    m_rows = x.shape[0]
    assert m_rows % tm == 0 and D_FF % tf == 0
    vmem = (4 * _nbytes((tm, D_MODEL), F32) + _nbytes((tm, D_MODEL), BF16) + _nbytes((tm, D_MODEL), F32)
            + 4 * _nbytes((D_MODEL, tf), BF16) + 3 * _nbytes((tm, tf), F32) + 2 * _nbytes((tm, D_MODEL), F32))
    return pl.pallas_call(
        _mlp_kernel,
        grid=(m_rows // tm, D_FF // tf),
        in_specs=[pl.BlockSpec((tm, D_MODEL), lambda m, f: (m, 0)), pl.BlockSpec((1, D_MODEL), lambda m, f: (0, 0)),
                  pl.BlockSpec((D_MODEL, tf), lambda m, f: (0, f)), pl.BlockSpec((tf, D_MODEL), lambda m, f: (f, 0)),
                  pl.BlockSpec((1, D_MODEL), lambda m, f: (0, 0))],
        out_specs=pl.BlockSpec((tm, D_MODEL), lambda m, f: (m, 0)),
        out_shape=jax.ShapeDtypeStruct((m_rows, D_MODEL), F32),
        scratch_shapes=[pltpu.VMEM((tm, D_MODEL), BF16), pltpu.VMEM((tm, D_MODEL), F32)],
        compiler_params=_params(("parallel", "arbitrary"), vmem),
        name="mlp",
    )(x, gpre, wup, wdn, gpost)


_PROMPT_TILES = dict(in_tm=1024, in_tn=512, lru_tt=512, attn_tq=256, attn_tk=512, merge_tm=256, mlp_tm=512, mlp_tf=1024)
_DECODE_PAGES_PER_STEP = 8


def _component_queries(q, dec_seq):
    b = q.shape[0]
    qt = q.reshape(b, dec_seq, N_HEADS, 2, HEAD_DIM).transpose(0, 2, 3, 1, 4)
    onehot = jnp.eye(2, dtype=q.dtype)
    qm = qt[:, :, :, :, None, :] * onehot[None, None, :, None, :, None]
    return qm.reshape(b, 2 * N_HEADS * dec_seq, V_DIM)


def kernel(x_prompt, x_sample, cache_k, cache_v, state_lru, state_conv, page_table, meta_tokens, g_pre_mix, w_in,
           lam_q1, lam_k1, lam_q2, lam_k2, g_subln, w_attn_br, conv_w, conv_b, w_rg, b_rg, w_ig, b_ig, lru_lambda,
           w_lru_br, w_out, g_post_mix, g_pre_mlp, w_up, w_down, g_post_mlp):
    depth = w_in.shape[0]
    assert depth == 1
    batch, seq, _ = x_prompt.shape
    dec_batch, dec_seq, _ = x_sample.shape
    tl = _PROMPT_TILES

    w_in_b = w_in[0].astype(BF16)
    w_attn_b = w_attn_br[0].astype(BF16)
    w_lru_b = w_lru_br[0].astype(BF16)
    w_out_b = w_out[0].astype(BF16)
    w_up_b = w_up[0].astype(BF16)
    w_down_b = w_down[0].astype(BF16)
    lp = dict(conv_w=conv_w[0], conv_b=conv_b, w_rg=w_rg[0].astype(BF16), b_rg=b_rg, w_ig=w_ig[0].astype(BF16),
              b_ig=b_ig, lam=lru_lambda)
    ap = dict(lam_q1=lam_q1, lam_k1=lam_k1, lam_q2=lam_q2, lam_k2=lam_k2, g_subln=g_subln)

    n_small = N_META + dec_batch * dec_seq
    x_small = jnp.concatenate([meta_tokens, x_sample.reshape(dec_batch * dec_seq, D_MODEL)], axis=0)
    q_sm, k_sm, v_sm, xr_sm, gr_sm, gates_sm = _in_proj(x_small, g_pre_mix, w_in_b, tm=n_small, tn=tl["in_tn"])

    zeros8 = jnp.zeros((1, SUBLANES, LRU_WIDTH), F32)
    _, h_meta, conv_meta = _lru(xr_sm[None, :N_META], gr_sm[None, :N_META], zeros8, jnp.zeros((1, 1, LRU_WIDTH), F32),
                                lp, tt=N_META, seq_start=True, t_valid=N_META)
    k_meta, v_meta = k_sm[:N_META], v_sm[:N_META]

    xp = x_prompt.reshape(batch * seq, D_MODEL)
    q_p, k_p, v_p, xr_p, gr_p, gates_p = _in_proj(xp, g_pre_mix, w_in_b, tm=tl["in_tm"], tn=tl["in_tn"])
    cbuf_p = jnp.broadcast_to(jnp.pad(conv_meta, ((0, 0), (SUBLANES - CONV_WIDTH + 1, 0), (0, 0))),
                              (batch, SUBLANES, LRU_WIDTH))
    hg_p, lru_prompt, conv_prompt = _lru(xr_p.reshape(batch, seq, LRU_WIDTH), gr_p.reshape(batch, seq, LRU_WIDTH),
                                         cbuf_p, jnp.broadcast_to(h_meta, (batch, 1, LRU_WIDTH)), lp,
                                         tt=tl["lru_tt"], seq_start=False, t_valid=seq)
    pad_meta = ((0, LANES - N_META), (0, 0))
    o_p = _attn_prompt(q_p.reshape(batch, seq, ATTN_WIDTH), k_p.reshape(batch, seq, ATTN_WIDTH),
                       v_p.reshape(batch, seq, V_WIDTH), jnp.pad(k_meta, pad_meta), jnp.pad(v_meta, pad_meta), ap,
                       tq=tl["attn_tq"], tk=tl["attn_tk"])
    x1_p = _merge(o_p.reshape(batch * seq, V_WIDTH), hg_p.reshape(batch * seq, LRU_WIDTH), gates_p, xp,
                  w_attn_b, w_lru_b, w_out_b, g_post_mix, tm=tl["merge_tm"])
    y_prompt = _mlp(x1_p, g_pre_mlp, w_up_b, w_down_b, g_post_mlp, tm=tl["mlp_tm"], tf=tl["mlp_tf"])

    n_s = dec_batch * dec_seq
    xs = x_sample.reshape(n_s, D_MODEL)
    q_s, k_s, v_s = q_sm[N_META:], k_sm[N_META:], v_sm[N_META:]
    pad_t = ((0, 0), (0, SUBLANES - dec_seq), (0, 0))
    cbuf_s = jnp.pad(state_conv[0], ((0, 0), (SUBLANES - CONV_WIDTH + 1, 0), (0, 0)))
    hg_s, lru_sample, conv_sample = _lru(jnp.pad(xr_sm[N_META:].reshape(dec_batch, dec_seq, LRU_WIDTH), pad_t),
                                         jnp.pad(gr_sm[N_META:].reshape(dec_batch, dec_seq, LRU_WIDTH), pad_t),
                                         cbuf_s, state_lru[0][:, None, :], lp,
                                         tt=SUBLANES, seq_start=False, t_valid=dec_seq)
    pad_new = ((0, 0), (0, LANES - dec_seq * N_HEADS), (0, 0))
    o_s = _attn_sample(_component_queries(q_s.reshape(dec_batch, dec_seq, ATTN_WIDTH), dec_seq),
                       jnp.pad(k_s.astype(BF16).reshape(dec_batch, dec_seq * N_HEADS, V_DIM), pad_new),
                       jnp.pad(v_s.astype(BF16).reshape(dec_batch, dec_seq * N_HEADS, V_DIM), pad_new),
                       cache_k, cache_v, page_table, ap, n_pg=_DECODE_PAGES_PER_STEP, dec_seq=dec_seq)
    o_s = o_s.reshape(dec_batch, N_HEADS, 2, dec_seq, V_DIM)[:, :, 0].transpose(0, 2, 1, 3)
    o_s = o_s.reshape(n_s, V_WIDTH).astype(BF16)
    x1_s = _merge(o_s, hg_s[:, :dec_seq].reshape(n_s, LRU_WIDTH), gates_sm[N_META:], xs,
                  w_attn_b, w_lru_b, w_out_b, g_post_mix, tm=n_s)
    y_sample = _mlp(x1_s, g_pre_mlp, w_up_b, w_down_b, g_post_mlp, tm=n_s, tf=tl["mlp_tf"])

    def with_meta(meta_rows, main, width):
        full = jnp.concatenate([jnp.broadcast_to(meta_rows[None], (batch, N_META, width)),
                                main.reshape(batch, seq, width)], axis=1)
        return full.reshape(1, batch, N_META + seq, N_HEADS, width // N_HEADS)

    return (y_prompt.reshape(batch, seq, D_MODEL),
            y_sample.reshape(dec_batch, dec_seq, D_MODEL),
            with_meta(k_meta, k_p, ATTN_WIDTH),
            with_meta(v_meta, v_p, V_WIDTH),
            lru_prompt.reshape(1, batch, LRU_WIDTH),
            conv_prompt.reshape(1, batch, CONV_WIDTH - 1, LRU_WIDTH),
            k_s.reshape(1, dec_batch, dec_seq, N_HEADS, 2 * HEAD_DIM),
            v_s.reshape(1, dec_batch, dec_seq, N_HEADS, V_DIM),
            lru_sample.reshape(1, dec_batch, LRU_WIDTH),
            conv_sample.reshape(1, dec_batch, CONV_WIDTH - 1, LRU_WIDTH))
```

```python
import functools
import math

import jax
import jax.numpy as jnp
from jax import lax
from jax.experimental import pallas as pl
from jax.experimental.pallas import tpu as pltpu

F32 = jnp.float32
BF16 = jnp.bfloat16

D_MODEL = 2048
N_META = 16
N_HEADS = 8
HEAD_DIM = 64
V_DIM = 2 * HEAD_DIM
ATTN_WIDTH = N_HEADS * 2 * HEAD_DIM
V_WIDTH = N_HEADS * V_DIM
LRU_WIDTH = D_MODEL // 2
N_LRU_BLOCKS = 8
LRU_BLOCK = LRU_WIDTH // N_LRU_BLOCKS
CONV_WIDTH = 4
LRU_C = 8.0
D_FF = 4 * D_MODEL
PAGE_SIZE = 128
NEG = -1e30
EPS = 1e-6
LAMBDA_INIT = 0.8 - 0.6 * math.exp(-0.3 * 0)
IN_WIDTH = 2 * ATTN_WIDTH + V_WIDTH + 2 * LRU_WIDTH + 2 * D_MODEL

SUBLANES = 8
LANES = 128
V7X_VMEM_CAP_BYTES = 60 * 1024 * 1024

Q_COL = 0
K_COL = Q_COL + ATTN_WIDTH
V_COL = K_COL + ATTN_WIDTH
XR_COL = V_COL + V_WIDTH
GR_COL = XR_COL + LRU_WIDTH
GA_COL = GR_COL + LRU_WIDTH
GL_COL = GA_COL + D_MODEL
assert GL_COL + D_MODEL == IN_WIDTH


def _nbytes(shape, dtype):
    return math.prod(shape) * jnp.dtype(dtype).itemsize


def _params(semantics, vmem_bytes):
    return pltpu.CompilerParams(dimension_semantics=semantics,
                                vmem_limit_bytes=min(int(vmem_bytes), V7X_VMEM_CAP_BYTES))


def _rms(x, g):
    return x * lax.rsqrt(jnp.mean(x * x, axis=-1, keepdims=True) + EPS) * g


def _diff_lambda(lq1_ref, lk1_ref, lq2_ref, lk2_ref):
    a = jnp.sum(lq1_ref[...] * lk1_ref[...], axis=1, keepdims=True)
    b = jnp.sum(lq2_ref[...] * lk2_ref[...], axis=1, keepdims=True)
    return jnp.exp(a) - jnp.exp(b) + LAMBDA_INIT


def _in_proj_kernel(x_ref, g_ref, w_ref, z_ref, h_scr):
    @pl.when(pl.program_id(1) == 0)
    def _():
        h_scr[...] = _rms(x_ref[...], g_ref[...]).astype(BF16)

    z_ref[...] = jnp.dot(h_scr[...], w_ref[...], preferred_element_type=F32)


def _in_proj(x, g, w, *, tm, tn):
    m_rows = x.shape[0]
    assert m_rows % tm == 0 and IN_WIDTH % tn == 0
    vmem = (2 * _nbytes((tm, D_MODEL), F32) + _nbytes((tm, D_MODEL), BF16) + 2 * _nbytes((D_MODEL, tn), BF16)
            + 3 * _nbytes((tm, tn), F32) + _nbytes((tm, D_MODEL), F32))
    return pl.pallas_call(
        _in_proj_kernel,
        grid=(m_rows // tm, IN_WIDTH // tn),
        in_specs=[pl.BlockSpec((tm, D_MODEL), lambda m, n: (m, 0)),
                  pl.BlockSpec((1, D_MODEL), lambda m, n: (0, 0)),
                  pl.BlockSpec((D_MODEL, tn), lambda m, n: (0, n))],
        out_specs=pl.BlockSpec((tm, tn), lambda m, n: (m, n)),
        out_shape=jax.ShapeDtypeStruct((m_rows, IN_WIDTH), F32),
        scratch_shapes=[pltpu.VMEM((tm, D_MODEL), BF16)],
        compiler_params=_params(("parallel", "arbitrary"), vmem),
        name="in_proj",
    )(x, g, w)


def _lru_kernel(xr_ref, gr_ref, cbuf_ref, h0_ref, cw_ref, cb_ref, wrg_ref, brg_ref, wig_ref, big_ref, lam_ref,
                hg_ref, hlast_ref, cnew_ref, ext_scr, a_scr, u_scr, h_scr, *, tt, seq_start, t_last):
    c = pl.program_id(1)

    @pl.when(c == 0)
    def _():
        ext_scr[0:SUBLANES, :] = cbuf_ref[0]
        h_scr[...] = h0_ref[0]

    def rot8(x, s):
        return pltpu.roll(x.reshape(tt // SUBLANES, SUBLANES, LRU_WIDTH), s, axis=1).reshape(tt, LRU_WIDTH)

    row = lax.broadcasted_iota(jnp.int32, (tt, LRU_WIDTH), 0)
    rmod = row & (SUBLANES - 1)
    xr = xr_ref[0]
    ext_scr[SUBLANES:SUBLANES + tt, :] = xr
    x_prev8 = ext_scr[0:tt, :]
    cw = cw_ref[...]
    xc = cb_ref[...] + xr * cw[CONV_WIDTH - 1:CONV_WIDTH]
    for s in range(1, CONV_WIDTH):
        shifted = rot8(jnp.where(rmod >= SUBLANES - s, x_prev8, xr), s)
        xc = xc + shifted * cw[CONV_WIDTH - 1 - s:CONV_WIDTH - s]
    ext_scr[0:SUBLANES, :] = ext_scr[tt:tt + SUBLANES, :]

    xcb = xc.astype(BF16)

    def gate(w_ref, b_ref):
        parts = [jnp.dot(xcb[:, n * LRU_BLOCK:(n + 1) * LRU_BLOCK], w_ref[n], preferred_element_type=F32)
                 for n in range(N_LRU_BLOCKS)]
        return jax.nn.sigmoid(jnp.concatenate(parts, axis=1) + b_ref[...])

    r = gate(wrg_ref, brg_ref)
    i = gate(wig_ref, big_ref)
    lam = lam_ref[...]
    softplus_neg_lam = jnp.maximum(-lam, 0.0) + jnp.log1p(jnp.exp(-jnp.abs(lam)))
    log_a = (-LRU_C) * r * softplus_neg_lam
    a = jnp.exp(log_a)
    mult = jnp.sqrt(-jnp.tanh(log_a) * (a * a + 1.0))
    if seq_start:
        mult = jnp.where(row + c * tt == 0, 1.0, mult)
    u = mult * i * xc

    for s in (1, 2, 4):
        ok = rmod >= s
        a_prev = rot8(a, s)
        u_prev = rot8(u, s)
        u = jnp.where(ok, a * u_prev + u, u)
        a = jnp.where(ok, a * a_prev, a)
    a_scr[...] = a
    u_scr[...] = u

    def carry(gidx, h):
        rows = pl.ds(pl.multiple_of(gidx * SUBLANES, SUBLANES), SUBLANES)
        hrows = a_scr[rows, :] * h + u_scr[rows, :]
        u_scr[rows, :] = hrows
        return hrows[SUBLANES - 1:SUBLANES, :]

    h_scr[...] = lax.fori_loop(0, tt // SUBLANES, carry, h_scr[...])
    hg_ref[0] = (u_scr[...] * jax.nn.gelu(gr_ref[0])).astype(BF16)

    @pl.when(c == pl.num_programs(1) - 1)
    def _():
        hlast_ref[0] = u_scr[t_last:t_last + 1, :]
        cnew_ref[0] = ext_scr[SUBLANES + t_last - 2:SUBLANES + t_last + 1, :]


def _lru(xr, gr, cbuf, h0, lp, *, tt, seq_start, t_valid, xr_col=0, gr_col=0):
    b, t, _ = xr.shape
    w = LRU_WIDTH
    assert t % tt == 0 and tt % SUBLANES == 0 and (t_valid - 1) // tt == t // tt - 1 and t_valid >= CONV_WIDTH - 1
    t_last = (t_valid - 1) % tt
    assert t_last >= CONV_WIDTH - 2
    seq = lambda bi, c: (bi, c, 0)
    per_b = lambda bi, c: (bi, 0, 0)
    const2 = lambda bi, c: (0, 0)
    const3 = lambda bi, c: (0, 0, 0)
    vmem = (4 * _nbytes((tt, w), F32) + 2 * _nbytes((tt, w), BF16) + 3 * _nbytes((tt + SUBLANES, w), F32)
            + 12 * _nbytes((tt, w), F32) + 4 * _nbytes((N_LRU_BLOCKS, LRU_BLOCK, LRU_BLOCK), BF16))
    return pl.pallas_call(
        functools.partial(_lru_kernel, tt=tt, seq_start=seq_start, t_last=t_last),
        grid=(b, t // tt),
        in_specs=[pl.BlockSpec((1, tt, w), lambda bi, c: (bi, c, xr_col)),
                  pl.BlockSpec((1, tt, w), lambda bi, c: (bi, c, gr_col)),
                  pl.BlockSpec((1, SUBLANES, w), per_b), pl.BlockSpec((1, 1, w), per_b),
                  pl.BlockSpec((CONV_WIDTH, w), const2), pl.BlockSpec((1, w), const2),
                  pl.BlockSpec((N_LRU_BLOCKS, LRU_BLOCK, LRU_BLOCK), const3), pl.BlockSpec((1, w), const2),
                  pl.BlockSpec((N_LRU_BLOCKS, LRU_BLOCK, LRU_BLOCK), const3), pl.BlockSpec((1, w), const2),
                  pl.BlockSpec((1, w), const2)],
        out_specs=[pl.BlockSpec((1, tt, w), seq), pl.BlockSpec((1, 1, w), per_b),
                   pl.BlockSpec((1, CONV_WIDTH - 1, w), per_b)],
        out_shape=[jax.ShapeDtypeStruct((b, t, w), BF16), jax.ShapeDtypeStruct((b, 1, w), F32),
                   jax.ShapeDtypeStruct((b, CONV_WIDTH - 1, w), F32)],
        scratch_shapes=[pltpu.VMEM((tt + SUBLANES, w), F32), pltpu.VMEM((tt, w), F32), pltpu.VMEM((tt, w), F32),
                        pltpu.VMEM((1, w), F32)],
        compiler_params=_params(("parallel", "arbitrary"), vmem),
        name="conv_rglru",
    )(xr, gr, cbuf, h0, lp["conv_w"], lp["conv_b"], lp["w_rg"], lp["b_rg"], lp["w_ig"], lp["b_ig"], lp["lam"])


def _attn_kernel(q_ref, k_ref, v_ref, km_ref, vm_ref, lq1_ref, lk1_ref, lq2_ref, lk2_ref, g_ref, o_ref,
                 kb_scr, vt_scr, *, tq, tk):
    t = k_ref.shape[1]
    n_pad = km_ref.shape[0]
    kb_scr[0:t, :] = k_ref[0].astype(BF16)
    kb_scr[t:t + n_pad, :] = km_ref[...].astype(BF16)
    vt_scr[:, 0:t] = v_ref[0].T.astype(BF16)
    vt_scr[:, t:t + n_pad] = vm_ref[...].T.astype(BF16)
    lam = _diff_lambda(lq1_ref, lk1_ref, lq2_ref, lk2_ref)
    g = g_ref[...]
    lane = lax.broadcasted_iota(jnp.int32, (tq, V_DIM), 1)

    def update(state, qqt, kc, vtc, mask):
        m_old, l_old, acc = state
        st = jnp.dot(kc, qqt, preferred_element_type=F32)
        if mask is not None:
            st = jnp.where(mask, st, NEG)
        m_new = jnp.maximum(m_old, jnp.max(st, axis=0, keepdims=True))
        alpha = jnp.exp(m_old - m_new)
        p = jnp.exp(st - m_new)
        l_new = alpha * l_old + jnp.sum(p, axis=0, keepdims=True)
        acc = alpha * acc + jnp.dot(vtc, p.astype(BF16), preferred_element_type=F32)
        return m_new, l_new, acc

    for qi in range(t // tq):
        q = q_ref[0, qi * tq:(qi + 1) * tq, :] * (HEAD_DIM ** -0.5)
        qq = jnp.concatenate([jnp.where(lane < HEAD_DIM, q, 0.0), jnp.where(lane >= HEAD_DIM, q, 0.0)], axis=0)
        qqt = qq.T.astype(BF16)
        state = (jnp.full((1, 2 * tq), NEG, F32), jnp.zeros((1, 2 * tq), F32), jnp.zeros((V_DIM, 2 * tq), F32))
        n_keys = (qi + 1) * tq
        for ks in range(0, n_keys, tk):
            w = min(tk, n_keys - ks)
            if ks + w < n_keys:
                state = update(state, qqt, kb_scr[ks:ks + w, :], vt_scr[:, ks:ks + w], None)
                continue
            r = lax.broadcasted_iota(jnp.int32, (w + n_pad, 2 * tq), 0)
            qpos = qi * tq + (lax.broadcasted_iota(jnp.int32, (w + n_pad, 2 * tq), 1) & (tq - 1))
            key = jnp.where(r < w, ks + r, jnp.where(r < w + N_META, 0, t))
            kc = jnp.concatenate([kb_scr[ks:ks + w, :], kb_scr[t:t + n_pad, :]], axis=0)
            vtc = jnp.concatenate([vt_scr[:, ks:ks + w], vt_scr[:, t:t + n_pad]], axis=1)
            state = update(state, qqt, kc, vtc, key <= qpos)
        _, l_fin, acc = state
        oa = acc / l_fin
        ot = oa[:, 0:tq] - lam * oa[:, tq:2 * tq]
        o_ref[0, qi * tq:(qi + 1) * tq, :] = (_rms(ot.T, g) * (1.0 - LAMBDA_INIT)).astype(BF16)


def _attn_prompt(z, k_meta, v_meta, ap, *, tq, tk):
    b, t, _ = z.shape
    assert t % tq == 0 and tk % tq == 0 and tq & (tq - 1) == 0
    hd = lambda bi, h: (bi, 0, h)
    col = lambda c0: (lambda bi, h: (bi, 0, c0 // V_DIM + h))
    meta = lambda bi, h: (0, h)
    const = lambda bi, h: (0, 0)
    vmem = (6 * _nbytes((t, V_DIM), F32) + 2 * _nbytes((t, V_DIM), BF16) + 4 * _nbytes((LANES, V_DIM), F32)
            + 2 * _nbytes((t + LANES, V_DIM), BF16) + 2 * _nbytes((t, V_DIM), F32)
            + 8 * _nbytes((tk + LANES, 2 * tq), F32))
    return pl.pallas_call(
        functools.partial(_attn_kernel, tq=tq, tk=tk),
        grid=(b, N_HEADS),
        in_specs=[pl.BlockSpec((1, t, V_DIM), col(Q_COL)), pl.BlockSpec((1, t, V_DIM), col(K_COL)),
                  pl.BlockSpec((1, t, V_DIM), col(V_COL)),
                  pl.BlockSpec((LANES, V_DIM), meta), pl.BlockSpec((LANES, V_DIM), meta),
                  pl.BlockSpec((1, HEAD_DIM), const), pl.BlockSpec((1, HEAD_DIM), const),
                  pl.BlockSpec((1, HEAD_DIM), const), pl.BlockSpec((1, HEAD_DIM), const),
                  pl.BlockSpec((1, V_DIM), const)],
        out_specs=pl.BlockSpec((1, t, V_DIM), hd),
        out_shape=jax.ShapeDtypeStruct((b, t, V_WIDTH), BF16),
        scratch_shapes=[pltpu.VMEM((t + LANES, V_DIM), BF16), pltpu.VMEM((V_DIM, t + LANES), BF16)],
        compiler_params=_params(("parallel", "parallel"), vmem),
        name="attn_prompt",
    )(z, z, z, k_meta, v_meta, ap["lam_q1"], ap["lam_k1"], ap["lam_q2"], ap["lam_k2"], ap["g_subln"])


def _decode_kernel(pt_ref, qm_ref, kn_ref, vn_ref, *rest, n_pg, dec_seq):
    del pt_ref
    k_refs, v_refs = rest[:n_pg], rest[n_pg:2 * n_pg]
    (lq1_ref, lk1_ref, lq2_ref, lk2_ref, g_ref, o_ref, m_scr, l_scr, acc_scr) = rest[2 * n_pg:]
    s_idx = pl.program_id(1)
    n_rows = 2 * N_HEADS * dec_seq
    page_rows = PAGE_SIZE * N_HEADS
    qm = qm_ref[0] * (HEAD_DIM ** -0.5)

    def update(kbs, vbs, bias):
        ss = [lax.dot_general(qm, kb, (((1,), (1,)), ((), ())), preferred_element_type=F32) + bias for kb in kbs]
        m_old = m_scr[...]
        m_new = m_old
        for s in ss:
            m_new = jnp.maximum(m_new, jnp.max(s, axis=1, keepdims=True))
        alpha = jnp.exp(m_old - m_new)
        l_new = alpha * l_scr[...]
        acc = alpha * acc_scr[...]
        for s, vb in zip(ss, vbs):
            p = jnp.exp(s - m_new)
            l_new = l_new + jnp.sum(p, axis=1, keepdims=True)
            acc = acc + jnp.dot(p, vb, preferred_element_type=F32)
        m_scr[...] = m_new
        l_scr[...] = l_new
        acc_scr[...] = acc

    def head_bias(width, extra=None):
        row = lax.broadcasted_iota(jnp.int32, (n_rows, width), 0)
        col = lax.broadcasted_iota(jnp.int32, (n_rows, width), 1)
        ok = (col & (N_HEADS - 1)) == row // (2 * dec_seq)
        if extra is not None:
            ok = ok & extra(row, col)
        return jnp.where(ok, 0.0, NEG)

    @pl.when(s_idx == 0)
    def _():
        m_scr[...] = jnp.full(m_scr.shape, NEG, F32)
        l_scr[...] = jnp.zeros(l_scr.shape, F32)
        acc_scr[...] = jnp.zeros(acc_scr.shape, F32)
        causal = lambda row, col: col // N_HEADS <= (row & (dec_seq - 1))
        update([kn_ref[0]], [vn_ref[0]], head_bias(kn_ref.shape[1], causal))

    update([k_refs[p][0, 0].reshape(page_rows, V_DIM) for p in range(n_pg)],
           [v_refs[p][0, 0].reshape(page_rows, V_DIM) for p in range(n_pg)], head_bias(page_rows))

    @pl.when(s_idx == pl.num_programs(1) - 1)
    def _():
        lam = _diff_lambda(lq1_ref, lk1_ref, lq2_ref, lk2_ref)
        oa = acc_scr[...] / l_scr[...]
        d = oa - lam * pltpu.roll(oa, n_rows - dec_seq, axis=0)
        o_ref[0] = _rms(d, g_ref[...]) * (1.0 - LAMBDA_INIT)


def _attn_sample(qm, k_new, v_new, cache_k, cache_v, page_table, ap, *, n_pg, dec_seq):
    b, n_pages = page_table.shape
    assert n_pages % n_pg == 0 and dec_seq & (dec_seq - 1) == 0 and N_HEADS & (N_HEADS - 1) == 0
    n_rows = 2 * N_HEADS * dec_seq
    page_rows = PAGE_SIZE * N_HEADS
    per_b = lambda bi, s, pt: (bi, 0, 0)
    const = lambda bi, s, pt: (0, 0)
    page = lambda p: (lambda bi, s, pt: (0, pt[bi, s * n_pg + p], 0, 0, 0))
    vmem = (4 * n_pg * _nbytes((page_rows, V_DIM), F32) + 6 * n_pg * _nbytes((n_rows, page_rows), F32)
            + 8 * _nbytes((LANES, V_DIM), F32))
    grid_spec = pltpu.PrefetchScalarGridSpec(
        num_scalar_prefetch=1,
        grid=(b, n_pages // n_pg),
        in_specs=([pl.BlockSpec((1, n_rows, V_DIM), per_b), pl.BlockSpec((1, LANES, V_DIM), per_b),
                   pl.BlockSpec((1, LANES, V_DIM), per_b)]
                  + [pl.BlockSpec((1, 1, PAGE_SIZE, N_HEADS, V_DIM), page(p)) for p in range(n_pg)]
                  + [pl.BlockSpec((1, 1, PAGE_SIZE, N_HEADS, V_DIM), page(p)) for p in range(n_pg)]
                  + [pl.BlockSpec((1, HEAD_DIM), const)] * 4 + [pl.BlockSpec((1, V_DIM), const)]),
        out_specs=pl.BlockSpec((1, n_rows, V_DIM), per_b),
        scratch_shapes=[pltpu.VMEM((n_rows, 1), F32), pltpu.VMEM((n_rows, 1), F32),
                        pltpu.VMEM((n_rows, V_DIM), F32)],
    )
    return pl.pallas_call(
        functools.partial(_decode_kernel, n_pg=n_pg, dec_seq=dec_seq),
        grid_spec=grid_spec,
        out_shape=jax.ShapeDtypeStruct((b, n_rows, V_DIM), F32),
        compiler_params=_params(("parallel", "arbitrary"), vmem),
        name="attn_sample",
    )(page_table, qm, k_new, v_new, *([cache_k] * n_pg), *([cache_v] * n_pg),
      ap["lam_q1"], ap["lam_k1"], ap["lam_q2"], ap["lam_k2"], ap["g_subln"])


def _merge_kernel(o_ref, hg_ref, ga0_ref, ga1_ref, gl0_ref, gl1_ref, x_ref, wa_ref, wl_ref, wo_ref, g_ref, out_ref):
    ya = jnp.dot(o_ref[...], wa_ref[...], preferred_element_type=F32)
    yl = jnp.dot(hg_ref[...], wl_ref[...], preferred_element_type=F32)
    half = D_MODEL // 2
    merged = jnp.concatenate(
        [jax.nn.sigmoid(ga0_ref[...]) * ya[:, :half] + jax.nn.sigmoid(gl0_ref[...]) * yl[:, :half],
         jax.nn.sigmoid(ga1_ref[...]) * ya[:, half:] + jax.nn.sigmoid(gl1_ref[...]) * yl[:, half:]], axis=1)
    mo = jnp.dot(merged.astype(BF16), wo_ref[...], preferred_element_type=F32)
    out_ref[...] = x_ref[...] + _rms(mo, g_ref[...])


def _merge(o, hg, gates, x, wa, wl, wo, g, *, tm, ga_col):
    m_rows = x.shape[0]
    half = D_MODEL // 2
    assert m_rows % tm == 0 and ga_col % half == 0
    row = lambda m: (m, 0)
    const = lambda m: (0, 0)
    gate = lambda j: pl.BlockSpec((tm, half), lambda m: (m, ga_col // half + j))
    vmem = (2 * (2 * _nbytes((tm, V_WIDTH), BF16) + 4 * _nbytes((tm, D_MODEL), F32))
            + 2 * (2 * _nbytes((V_WIDTH, D_MODEL), BF16) + _nbytes((D_MODEL, D_MODEL), BF16))
            + 6 * _nbytes((tm, D_MODEL), F32))
    return pl.pallas_call(
        _merge_kernel,
        grid=(m_rows // tm,),
        in_specs=[pl.BlockSpec((tm, V_WIDTH), row), pl.BlockSpec((tm, LRU_WIDTH), row),
                  gate(0), gate(1), gate(2), gate(3),
                  pl.BlockSpec((tm, D_MODEL), row),
                  pl.BlockSpec((V_WIDTH, D_MODEL), const), pl.BlockSpec((LRU_WIDTH, D_MODEL), const),
                  pl.BlockSpec((D_MODEL, D_MODEL), const), pl.BlockSpec((1, D_MODEL), const)],
        out_specs=pl.BlockSpec((tm, D_MODEL), row),
        out_shape=jax.ShapeDtypeStruct((m_rows, D_MODEL), F32),
        compiler_params=_params(("parallel",), vmem),
        name="merge_out_proj",
    )(o, hg, gates, gates, gates, gates, x, wa, wl, wo, g)


def _mlp_kernel(x_ref, gpre_ref, wup_ref, wdn_ref, gpost_ref, out_ref, hn_scr, acc_scr):
    f = pl.program_id(1)

    @pl.when(f == 0)
    def _():
        hn_scr[...] = _rms(x_ref[...], gpre_ref[...]).astype(BF16)
        acc_scr[...] = jnp.zeros(acc_scr.shape, F32)

    u = jnp.square(jnp.maximum(jnp.dot(hn_scr[...], wup_ref[...], preferred_element_type=F32), 0.0))
    acc_scr[...] += jnp.dot(u.astype(BF16), wdn_ref[...], preferred_element_type=F32)

    @pl.when(f == pl.num_programs(1) - 1)
    def _():
        out_ref[...] = x_ref[...] + _rms(acc_scr[...], gpost_ref[...])


def _mlp(x, gpre, wup, wdn, gpost, *, tm, tf):
    m_rows = x.shape[0]
    assert m_rows % tm == 0 and D_FF % tf == 0
    vmem = (4 * _nbytes((tm, D_MODEL), F32) + _nbytes((tm, D_MODEL), BF16) + _nbytes((tm, D_MODEL), F32)
            + 4 * _nbytes((D_MODEL, tf), BF16) + 3 * _nbytes((tm, tf), F32) + 2 * _nbytes((tm, D_MODEL), F32))
    return pl.pallas_call(
        _mlp_kernel,
        grid=(m_rows // tm, D_FF // tf),
        in_specs=[pl.BlockSpec((tm, D_MODEL), lambda m, f: (m, 0)), pl.BlockSpec((1, D_MODEL), lambda m, f: (0, 0)),
                  pl.BlockSpec((D_MODEL, tf), lambda m, f: (0, f)), pl.BlockSpec((tf, D_MODEL), lambda m, f: (f, 0)),
                  pl.BlockSpec((1, D_MODEL), lambda m, f: (0, 0))],
        out_specs=pl.BlockSpec((tm, D_MODEL), lambda m, f: (m, 0)),
        out_shape=jax.ShapeDtypeStruct((m_rows, D_MODEL), F32),
        scratch_shapes=[pltpu.VMEM((tm, D_MODEL), BF16), pltpu.VMEM((tm, D_MODEL), F32)],
        compiler_params=_params(("parallel", "arbitrary"), vmem),
        name="mlp",
    )(x, gpre, wup, wdn, gpost)


_PROMPT_TILES = dict(in_tm=1024, in_tn=1024, lru_tt=512, attn_tq=256, attn_tk=512, merge_tm=256, mlp_tm=512, mlp_tf=1024)
_DECODE_PAGES_PER_STEP = 8


def _component_queries(q, dec_seq):
    b = q.shape[0]
    qt = q.reshape(b, dec_seq, N_HEADS, 2, HEAD_DIM).transpose(0, 2, 3, 1, 4)
    onehot = jnp.eye(2, dtype=q.dtype)
    qm = qt[:, :, :, :, None, :] * onehot[None, None, :, None, :, None]
    return qm.reshape(b, 2 * N_HEADS * dec_seq, V_DIM)


def kernel(x_prompt, x_sample, cache_k, cache_v, state_lru, state_conv, page_table, meta_tokens, g_pre_mix, w_in,
           lam_q1, lam_k1, lam_q2, lam_k2, g_subln, w_attn_br, conv_w, conv_b, w_rg, b_rg, w_ig, b_ig, lru_lambda,
           w_lru_br, w_out, g_post_mix, g_pre_mlp, w_up, w_down, g_post_mlp):
    depth = w_in.shape[0]
    assert depth == 1
    batch, seq, _ = x_prompt.shape
    dec_batch, dec_seq, _ = x_sample.shape
    tl = _PROMPT_TILES

    w_in_b = w_in[0].astype(BF16)
    w_attn_b = w_attn_br[0].astype(BF16)
    w_lru_b = w_lru_br[0].astype(BF16)
    w_out_b = w_out[0].astype(BF16)
    w_up_b = w_up[0].astype(BF16)
    w_down_b = w_down[0].astype(BF16)
    lp = dict(conv_w=conv_w[0], conv_b=conv_b, w_rg=w_rg[0].astype(BF16), b_rg=b_rg, w_ig=w_ig[0].astype(BF16),
              b_ig=b_ig, lam=lru_lambda)
    ap = dict(lam_q1=lam_q1, lam_k1=lam_k1, lam_q2=lam_q2, lam_k2=lam_k2, g_subln=g_subln)

    n_small = N_META + dec_batch * dec_seq
    x_small = jnp.concatenate([meta_tokens, x_sample.reshape(dec_batch * dec_seq, D_MODEL)], axis=0)
    z_sm = _in_proj(x_small, g_pre_mix, w_in_b, tm=n_small, tn=tl["in_tn"])
    z_meta, z_s = z_sm[:N_META], z_sm[N_META:]

    zeros8 = jnp.zeros((1, SUBLANES, LRU_WIDTH), F32)
    _, h_meta, conv_meta = _lru(z_meta[None], z_meta[None], zeros8, jnp.zeros((1, 1, LRU_WIDTH), F32), lp,
                                tt=N_META, seq_start=True, t_valid=N_META,
                                xr_col=XR_COL // LRU_WIDTH, gr_col=GR_COL // LRU_WIDTH)
    k_meta, v_meta = z_meta[:, K_COL:K_COL + ATTN_WIDTH], z_meta[:, V_COL:V_COL + V_WIDTH]

    xp = x_prompt.reshape(batch * seq, D_MODEL)
    z_p = _in_proj(xp, g_pre_mix, w_in_b, tm=tl["in_tm"], tn=tl["in_tn"])
    z_p3 = z_p.reshape(batch, seq, IN_WIDTH)
    cbuf_p = jnp.broadcast_to(jnp.pad(conv_meta, ((0, 0), (SUBLANES - CONV_WIDTH + 1, 0), (0, 0))),
                              (batch, SUBLANES, LRU_WIDTH))
    hg_p, lru_prompt, conv_prompt = _lru(z_p3, z_p3, cbuf_p, jnp.broadcast_to(h_meta, (batch, 1, LRU_WIDTH)), lp,
                                         tt=tl["lru_tt"], seq_start=False, t_valid=seq,
                                         xr_col=XR_COL // LRU_WIDTH, gr_col=GR_COL // LRU_WIDTH)
    pad_meta = ((0, LANES - N_META), (0, 0))
    o_p = _attn_prompt(z_p3, jnp.pad(k_meta, pad_meta), jnp.pad(v_meta, pad_meta), ap,
                       tq=tl["attn_tq"], tk=tl["attn_tk"])
    x1_p = _merge(o_p.reshape(batch * seq, V_WIDTH), hg_p.reshape(batch * seq, LRU_WIDTH), z_p, xp,
                  w_attn_b, w_lru_b, w_out_b, g_post_mix, tm=tl["merge_tm"], ga_col=GA_COL)
    y_prompt = _mlp(x1_p, g_pre_mlp, w_up_b, w_down_b, g_post_mlp, tm=tl["mlp_tm"], tf=tl["mlp_tf"])

    n_s = dec_batch * dec_seq
    xs = x_sample.reshape(n_s, D_MODEL)
    q_s, k_s, v_s = z_s[:, Q_COL:Q_COL + ATTN_WIDTH], z_s[:, K_COL:K_COL + ATTN_WIDTH], z_s[:, V_COL:V_COL + V_WIDTH]
    pad_t = ((0, 0), (0, SUBLANES - dec_seq), (0, 0))
    cbuf_s = jnp.pad(state_conv[0], ((0, 0), (SUBLANES - CONV_WIDTH + 1, 0), (0, 0)))
    hg_s, lru_sample, conv_sample = _lru(
        jnp.pad(z_s[:, XR_COL:XR_COL + LRU_WIDTH].reshape(dec_batch, dec_seq, LRU_WIDTH), pad_t),
        jnp.pad(z_s[:, GR_COL:GR_COL + LRU_WIDTH].reshape(dec_batch, dec_seq, LRU_WIDTH), pad_t),
        cbuf_s, state_lru[0][:, None, :], lp, tt=SUBLANES, seq_start=False, t_valid=dec_seq)
    pad_new = ((0, 0), (0, LANES - dec_seq * N_HEADS), (0, 0))
    o_s = _attn_sample(_component_queries(q_s.reshape(dec_batch, dec_seq, ATTN_WIDTH), dec_seq),
                       jnp.pad(k_s.reshape(dec_batch, dec_seq * N_HEADS, V_DIM), pad_new),
                       jnp.pad(v_s.reshape(dec_batch, dec_seq * N_HEADS, V_DIM), pad_new),
                       cache_k, cache_v, page_table, ap, n_pg=_DECODE_PAGES_PER_STEP, dec_seq=dec_seq)
    o_s = o_s.reshape(dec_batch, N_HEADS, 2, dec_seq, V_DIM)[:, :, 0].transpose(0, 2, 1, 3)
    o_s = o_s.reshape(n_s, V_WIDTH).astype(BF16)
    x1_s = _merge(o_s, hg_s[:, :dec_seq].reshape(n_s, LRU_WIDTH), z_s[:, GA_COL:], xs,
                  w_attn_b, w_lru_b, w_out_b, g_post_mix, tm=n_s, ga_col=0)
    y_sample = _mlp(x1_s, g_pre_mlp, w_up_b, w_down_b, g_post_mlp, tm=n_s, tf=tl["mlp_tf"])
    k_p, v_p = z_p3[:, :, K_COL:K_COL + ATTN_WIDTH], z_p3[:, :, V_COL:V_COL + V_WIDTH]

    def with_meta(meta_rows, main, width):
        full = jnp.concatenate([jnp.broadcast_to(meta_rows[None], (batch, N_META, width)),
                                main.reshape(batch, seq, width)], axis=1)
        return full.reshape(1, batch, N_META + seq, N_HEADS, width // N_HEADS)

    return (y_prompt.reshape(batch, seq, D_MODEL),
            y_sample.reshape(dec_batch, dec_seq, D_MODEL),
            with_meta(k_meta, k_p, ATTN_WIDTH),
            with_meta(v_meta, v_p, V_WIDTH),
            lru_prompt.reshape(1, batch, LRU_WIDTH),
            conv_prompt.reshape(1, batch, CONV_WIDTH - 1, LRU_WIDTH),
            k_s.reshape(1, dec_batch, dec_seq, N_HEADS, 2 * HEAD_DIM),
            v_s.reshape(1, dec_batch, dec_seq, N_HEADS, V_DIM),
            lru_sample.reshape(1, dec_batch, LRU_WIDTH),
            conv_sample.reshape(1, dec_batch, CONV_WIDTH - 1, LRU_WIDTH))
```

```python
import functools
import math

import jax
import jax.numpy as jnp
from jax import lax
from jax.experimental import pallas as pl
from jax.experimental.pallas import tpu as pltpu

F32 = jnp.float32
BF16 = jnp.bfloat16

D_MODEL = 2048
N_META = 16
N_HEADS = 8
HEAD_DIM = 64
V_DIM = 2 * HEAD_DIM
ATTN_WIDTH = N_HEADS * 2 * HEAD_DIM
V_WIDTH = N_HEADS * V_DIM
LRU_WIDTH = D_MODEL // 2
N_LRU_BLOCKS = 8
LRU_BLOCK = LRU_WIDTH // N_LRU_BLOCKS
CONV_WIDTH = 4
LRU_C = 8.0
D_FF = 4 * D_MODEL
PAGE_SIZE = 128
NEG = -1e30
EPS = 1e-6
LAMBDA_INIT = 0.8 - 0.6 * math.exp(-0.3 * 0)
IN_WIDTH = 2 * ATTN_WIDTH + V_WIDTH + 2 * LRU_WIDTH + 2 * D_MODEL

SUBLANES = 8
LANES = 128
V7X_VMEM_CAP_BYTES = 60 * 1024 * 1024

Q_COL = 0
K_COL = Q_COL + ATTN_WIDTH
V_COL = K_COL + ATTN_WIDTH
XR_COL = V_COL + V_WIDTH
GR_COL = XR_COL + LRU_WIDTH
GA_COL = GR_COL + LRU_WIDTH
GL_COL = GA_COL + D_MODEL
assert GL_COL + D_MODEL == IN_WIDTH


def _nbytes(shape, dtype):
    return math.prod(shape) * jnp.dtype(dtype).itemsize


def _params(semantics, vmem_bytes):
    return pltpu.CompilerParams(dimension_semantics=semantics,
                                vmem_limit_bytes=min(int(vmem_bytes), V7X_VMEM_CAP_BYTES))


def _rms(x, g):
    return x * lax.rsqrt(jnp.mean(x * x, axis=-1, keepdims=True) + EPS) * g


def _diff_lambda(lq1_ref, lk1_ref, lq2_ref, lk2_ref):
    a = jnp.sum(lq1_ref[...] * lk1_ref[...], axis=1, keepdims=True)
    b = jnp.sum(lq2_ref[...] * lk2_ref[...], axis=1, keepdims=True)
    return jnp.exp(a) - jnp.exp(b) + LAMBDA_INIT


def _in_proj_kernel(x_ref, g_ref, w_ref, z_ref, h_scr):
    @pl.when(pl.program_id(1) == 0)
    def _():
        h_scr[...] = _rms(x_ref[...], g_ref[...]).astype(BF16)

    z_ref[...] = jnp.dot(h_scr[...], w_ref[...], preferred_element_type=F32)


def _in_proj(x, g, w, *, tm, tn):
    m_rows = x.shape[0]
    assert m_rows % tm == 0 and IN_WIDTH % tn == 0
    vmem = (2 * _nbytes((tm, D_MODEL), F32) + _nbytes((tm, D_MODEL), BF16) + 2 * _nbytes((D_MODEL, tn), BF16)
            + 3 * _nbytes((tm, tn), F32) + _nbytes((tm, D_MODEL), F32))
    return pl.pallas_call(
        _in_proj_kernel,
        grid=(m_rows // tm, IN_WIDTH // tn),
        in_specs=[pl.BlockSpec((tm, D_MODEL), lambda m, n: (m, 0)),
                  pl.BlockSpec((1, D_MODEL), lambda m, n: (0, 0)),
                  pl.BlockSpec((D_MODEL, tn), lambda m, n: (0, n))],
        out_specs=pl.BlockSpec((tm, tn), lambda m, n: (m, n)),
        out_shape=jax.ShapeDtypeStruct((m_rows, IN_WIDTH), F32),
        scratch_shapes=[pltpu.VMEM((tm, D_MODEL), BF16)],
        compiler_params=_params(("parallel", "arbitrary"), vmem),
        name="in_proj",
    )(x, g, w)


def _lru_kernel(xr_ref, gr_ref, cbuf_ref, h0_ref, cw_ref, cb_ref, wrg_ref, brg_ref, wig_ref, big_ref, lam_ref,
                hg_ref, hlast_ref, cnew_ref, ext_scr, a_scr, u_scr, h_scr, *, tt, seq_start, t_last):
    c = pl.program_id(1)

    @pl.when(c == 0)
    def _():
        ext_scr[0:SUBLANES, :] = cbuf_ref[0]
        h_scr[...] = h0_ref[0]

    def rot8(x, s):
        return pltpu.roll(x.reshape(tt // SUBLANES, SUBLANES, LRU_WIDTH), s, axis=1).reshape(tt, LRU_WIDTH)

    row = lax.broadcasted_iota(jnp.int32, (tt, LRU_WIDTH), 0)
    rmod = row & (SUBLANES - 1)
    xr = xr_ref[0]
    ext_scr[SUBLANES:SUBLANES + tt, :] = xr
    x_prev8 = ext_scr[0:tt, :]
    cw = cw_ref[...]
    xc = cb_ref[...] + xr * cw[CONV_WIDTH - 1:CONV_WIDTH]
    for s in range(1, CONV_WIDTH):
        shifted = rot8(jnp.where(rmod >= SUBLANES - s, x_prev8, xr), s)
        xc = xc + shifted * cw[CONV_WIDTH - 1 - s:CONV_WIDTH - s]
    ext_scr[0:SUBLANES, :] = ext_scr[tt:tt + SUBLANES, :]

    xcb = xc.astype(BF16)

    def gate(w_ref, b_ref):
        parts = [jnp.dot(xcb[:, n * LRU_BLOCK:(n + 1) * LRU_BLOCK], w_ref[n], preferred_element_type=F32)
                 for n in range(N_LRU_BLOCKS)]
        return jax.nn.sigmoid(jnp.concatenate(parts, axis=1) + b_ref[...])

    r = gate(wrg_ref, brg_ref)
    i = gate(wig_ref, big_ref)
    lam = lam_ref[...]
    softplus_neg_lam = jnp.maximum(-lam, 0.0) + jnp.log1p(jnp.exp(-jnp.abs(lam)))
    log_a = (-LRU_C) * r * softplus_neg_lam
    a = jnp.exp(log_a)
    mult = jnp.sqrt(-jnp.tanh(log_a) * (a * a + 1.0))
    if seq_start:
        mult = jnp.where(row + c * tt == 0, 1.0, mult)
    u = mult * i * xc

    for s in (1, 2, 4):
        ok = rmod >= s
        a_prev = rot8(a, s)
        u_prev = rot8(u, s)
        u = jnp.where(ok, a * u_prev + u, u)
        a = jnp.where(ok, a * a_prev, a)
    a_scr[...] = a
    u_scr[...] = u

    def carry(gidx, h):
        rows = pl.ds(pl.multiple_of(gidx * SUBLANES, SUBLANES), SUBLANES)
        hrows = a_scr[rows, :] * h + u_scr[rows, :]
        u_scr[rows, :] = hrows
        return hrows[SUBLANES - 1:SUBLANES, :]

    h_scr[...] = lax.fori_loop(0, tt // SUBLANES, carry, h_scr[...])
    hg_ref[0] = (u_scr[...] * jax.nn.gelu(gr_ref[0])).astype(BF16)

    @pl.when(c == pl.num_programs(1) - 1)
    def _():
        hlast_ref[0] = u_scr[t_last:t_last + 1, :]
        cnew_ref[0] = ext_scr[SUBLANES + t_last - 2:SUBLANES + t_last + 1, :]


def _lru(xr, gr, cbuf, h0, lp, *, tt, seq_start, t_valid, xr_col=0, gr_col=0):
    b, t, _ = xr.shape
    w = LRU_WIDTH
    assert t % tt == 0 and tt % SUBLANES == 0 and (t_valid - 1) // tt == t // tt - 1 and t_valid >= CONV_WIDTH - 1
    t_last = (t_valid - 1) % tt
    assert t_last >= CONV_WIDTH - 2
    seq = lambda bi, c: (bi, c, 0)
    per_b = lambda bi, c: (bi, 0, 0)
    const2 = lambda bi, c: (0, 0)
    const3 = lambda bi, c: (0, 0, 0)
    vmem = (4 * _nbytes((tt, w), F32) + 2 * _nbytes((tt, w), BF16) + 3 * _nbytes((tt + SUBLANES, w), F32)
            + 12 * _nbytes((tt, w), F32) + 4 * _nbytes((N_LRU_BLOCKS, LRU_BLOCK, LRU_BLOCK), BF16))
    return pl.pallas_call(
        functools.partial(_lru_kernel, tt=tt, seq_start=seq_start, t_last=t_last),
        grid=(b, t // tt),
        in_specs=[pl.BlockSpec((1, tt, w), lambda bi, c: (bi, c, xr_col)),
                  pl.BlockSpec((1, tt, w), lambda bi, c: (bi, c, gr_col)),
                  pl.BlockSpec((1, SUBLANES, w), per_b), pl.BlockSpec((1, 1, w), per_b),
                  pl.BlockSpec((CONV_WIDTH, w), const2), pl.BlockSpec((1, w), const2),
                  pl.BlockSpec((N_LRU_BLOCKS, LRU_BLOCK, LRU_BLOCK), const3), pl.BlockSpec((1, w), const2),
                  pl.BlockSpec((N_LRU_BLOCKS, LRU_BLOCK, LRU_BLOCK), const3), pl.BlockSpec((1, w), const2),
                  pl.BlockSpec((1, w), const2)],
        out_specs=[pl.BlockSpec((1, tt, w), seq), pl.BlockSpec((1, 1, w), per_b),
                   pl.BlockSpec((1, CONV_WIDTH - 1, w), per_b)],
        out_shape=[jax.ShapeDtypeStruct((b, t, w), BF16), jax.ShapeDtypeStruct((b, 1, w), F32),
                   jax.ShapeDtypeStruct((b, CONV_WIDTH - 1, w), F32)],
        scratch_shapes=[pltpu.VMEM((tt + SUBLANES, w), F32), pltpu.VMEM((tt, w), F32), pltpu.VMEM((tt, w), F32),
                        pltpu.VMEM((1, w), F32)],
        compiler_params=_params(("parallel", "arbitrary"), vmem),
        name="conv_rglru",
    )(xr, gr, cbuf, h0, lp["conv_w"], lp["conv_b"], lp["w_rg"], lp["b_rg"], lp["w_ig"], lp["b_ig"], lp["lam"])


def _attn_kernel(q_ref, k_ref, v_ref, km_ref, vm_ref, lq1_ref, lk1_ref, lq2_ref, lk2_ref, g_ref, o_ref, ko_ref, vo_ref,
                 kb_scr, vt_scr, *, tq, tk):
    t = k_ref.shape[1]
    n_pad = km_ref.shape[0]
    ko_ref[0] = k_ref[0]
    vo_ref[0] = v_ref[0]
    kb_scr[0:t, :] = k_ref[0].astype(BF16)
    kb_scr[t:t + n_pad, :] = km_ref[...].astype(BF16)
    vt_scr[:, 0:t] = v_ref[0].T.astype(BF16)
    vt_scr[:, t:t + n_pad] = vm_ref[...].T.astype(BF16)
    lam = _diff_lambda(lq1_ref, lk1_ref, lq2_ref, lk2_ref)
    g = g_ref[...]
    lane = lax.broadcasted_iota(jnp.int32, (tq, V_DIM), 1)

    def update(state, qqt, kc, vtc, mask):
        m_old, l_old, acc = state
        st = jnp.dot(kc, qqt, preferred_element_type=F32)
        if mask is not None:
            st = jnp.where(mask, st, NEG)
        m_new = jnp.maximum(m_old, jnp.max(st, axis=0, keepdims=True))
        alpha = jnp.exp(m_old - m_new)
        p = jnp.exp(st - m_new)
        l_new = alpha * l_old + jnp.sum(p, axis=0, keepdims=True)
        acc = alpha * acc + jnp.dot(vtc, p.astype(BF16), preferred_element_type=F32)
        return m_new, l_new, acc

    for qi in range(t // tq):
        q = q_ref[0, qi * tq:(qi + 1) * tq, :] * (HEAD_DIM ** -0.5)
        qq = jnp.concatenate([jnp.where(lane < HEAD_DIM, q, 0.0), jnp.where(lane >= HEAD_DIM, q, 0.0)], axis=0)
        qqt = qq.T.astype(BF16)
        state = (jnp.full((1, 2 * tq), NEG, F32), jnp.zeros((1, 2 * tq), F32), jnp.zeros((V_DIM, 2 * tq), F32))
        n_keys = (qi + 1) * tq
        for ks in range(0, n_keys, tk):
            w = min(tk, n_keys - ks)
            if ks + w < n_keys:
                state = update(state, qqt, kb_scr[ks:ks + w, :], vt_scr[:, ks:ks + w], None)
                continue
            r = lax.broadcasted_iota(jnp.int32, (w + n_pad, 2 * tq), 0)
            qpos = qi * tq + (lax.broadcasted_iota(jnp.int32, (w + n_pad, 2 * tq), 1) & (tq - 1))
            key = jnp.where(r < w, ks + r, jnp.where(r < w + N_META, 0, t))
            kc = jnp.concatenate([kb_scr[ks:ks + w, :], kb_scr[t:t + n_pad, :]], axis=0)
            vtc = jnp.concatenate([vt_scr[:, ks:ks + w], vt_scr[:, t:t + n_pad]], axis=1)
            state = update(state, qqt, kc, vtc, key <= qpos)
        _, l_fin, acc = state
        oa = acc / l_fin
        ot = oa[:, 0:tq] - lam * oa[:, tq:2 * tq]
        o_ref[0, qi * tq:(qi + 1) * tq, :] = (_rms(ot.T, g) * (1.0 - LAMBDA_INIT)).astype(BF16)


def _attn_prompt(z, k_meta, v_meta, ap, *, tq, tk):
    b, t, _ = z.shape
    assert t % tq == 0 and tk % tq == 0 and tq & (tq - 1) == 0
    hd = lambda bi, h: (bi, 0, h)
    col = lambda c0: (lambda bi, h: (bi, 0, c0 // V_DIM + h))
    meta = lambda bi, h: (0, h)
    const = lambda bi, h: (0, 0)
    vmem = (10 * _nbytes((t, V_DIM), F32) + 2 * _nbytes((t, V_DIM), BF16) + 4 * _nbytes((LANES, V_DIM), F32)
            + 2 * _nbytes((t + LANES, V_DIM), BF16) + 2 * _nbytes((t, V_DIM), F32)
            + 8 * _nbytes((tk + LANES, 2 * tq), F32))
    return pl.pallas_call(
        functools.partial(_attn_kernel, tq=tq, tk=tk),
        grid=(b, N_HEADS),
        in_specs=[pl.BlockSpec((1, t, V_DIM), col(Q_COL)), pl.BlockSpec((1, t, V_DIM), col(K_COL)),
                  pl.BlockSpec((1, t, V_DIM), col(V_COL)),
                  pl.BlockSpec((LANES, V_DIM), meta), pl.BlockSpec((LANES, V_DIM), meta),
                  pl.BlockSpec((1, HEAD_DIM), const), pl.BlockSpec((1, HEAD_DIM), const),
                  pl.BlockSpec((1, HEAD_DIM), const), pl.BlockSpec((1, HEAD_DIM), const),
                  pl.BlockSpec((1, V_DIM), const)],
        out_specs=[pl.BlockSpec((1, t, V_DIM), hd)] * 3,
        out_shape=[jax.ShapeDtypeStruct((b, t, V_WIDTH), BF16), jax.ShapeDtypeStruct((b, t, ATTN_WIDTH), F32),
                   jax.ShapeDtypeStruct((b, t, V_WIDTH), F32)],
        scratch_shapes=[pltpu.VMEM((t + LANES, V_DIM), BF16), pltpu.VMEM((V_DIM, t + LANES), BF16)],
        compiler_params=_params(("parallel", "parallel"), vmem),
        name="attn_prompt",
    )(z, z, z, k_meta, v_meta, ap["lam_q1"], ap["lam_k1"], ap["lam_q2"], ap["lam_k2"], ap["g_subln"])


def _merge_kernel(o_ref, hg_ref, ga0_ref, ga1_ref, gl0_ref, gl1_ref, x_ref, wa_ref, wl_ref, wo_ref, g_ref, out_ref):
    ya = jnp.dot(o_ref[...], wa_ref[...], preferred_element_type=F32)
    yl = jnp.dot(hg_ref[...], wl_ref[...], preferred_element_type=F32)
    half = D_MODEL // 2
    merged = jnp.concatenate(
        [jax.nn.sigmoid(ga0_ref[...]) * ya[:, :half] + jax.nn.sigmoid(gl0_ref[...]) * yl[:, :half],
         jax.nn.sigmoid(ga1_ref[...]) * ya[:, half:] + jax.nn.sigmoid(gl1_ref[...]) * yl[:, half:]], axis=1)
    mo = jnp.dot(merged.astype(BF16), wo_ref[...], preferred_element_type=F32)
    out_ref[...] = x_ref[...] + _rms(mo, g_ref[...])


def _merge(o, hg, gates, x, wa, wl, wo, g, *, tm, ga_col):
    m_rows = x.shape[0]
    half = D_MODEL // 2
    assert m_rows % tm == 0 and ga_col % half == 0
    row = lambda m: (m, 0)
    const = lambda m: (0, 0)
    gate = lambda j: pl.BlockSpec((tm, half), lambda m: (m, ga_col // half + j))
    vmem = (2 * (2 * _nbytes((tm, V_WIDTH), BF16) + 4 * _nbytes((tm, D_MODEL), F32))
            + 2 * (2 * _nbytes((V_WIDTH, D_MODEL), BF16) + _nbytes((D_MODEL, D_MODEL), BF16))
            + 6 * _nbytes((tm, D_MODEL), F32))
    return pl.pallas_call(
        _merge_kernel,
        grid=(m_rows // tm,),
        in_specs=[pl.BlockSpec((tm, V_WIDTH), row), pl.BlockSpec((tm, LRU_WIDTH), row),
                  gate(0), gate(1), gate(2), gate(3),
                  pl.BlockSpec((tm, D_MODEL), row),
                  pl.BlockSpec((V_WIDTH, D_MODEL), const), pl.BlockSpec((LRU_WIDTH, D_MODEL), const),
                  pl.BlockSpec((D_MODEL, D_MODEL), const), pl.BlockSpec((1, D_MODEL), const)],
        out_specs=pl.BlockSpec((tm, D_MODEL), row),
        out_shape=jax.ShapeDtypeStruct((m_rows, D_MODEL), F32),
        compiler_params=_params(("parallel",), vmem),
        name="merge_out_proj",
    )(o, hg, gates, gates, gates, gates, x, wa, wl, wo, g)


def _mlp_phases(f, n_f, x_ref, gpre_ref, wup_ref, wdn_ref, gpost_ref, out_ref, hn_scr, acc_scr):
    def pre():
        @pl.when(f == 0)
        def _():
            hn_scr[...] = _rms(x_ref[...], gpre_ref[...]).astype(BF16)
            acc_scr[...] = jnp.zeros(acc_scr.shape, F32)

    def up():
        u = jnp.square(jnp.maximum(jnp.dot(hn_scr[...], wup_ref[...], preferred_element_type=F32), 0.0))
        return u.astype(BF16)

    def down(u):
        acc_scr[...] += jnp.dot(u, wdn_ref[...], preferred_element_type=F32)

    def post():
        @pl.when(f == n_f - 1)
        def _():
            out_ref[...] = x_ref[...] + _rms(acc_scr[...], gpost_ref[...])

    return pre, up, down, post


def _mlp_kernel(*refs):
    pre, up, down, post = _mlp_phases(pl.program_id(1), pl.num_programs(1), *refs)
    pre()
    down(up())
    post()


def _mlp_vmem(tm, tf):
    return (4 * _nbytes((tm, D_MODEL), F32) + _nbytes((tm, D_MODEL), BF16) + _nbytes((tm, D_MODEL), F32)
            + 4 * _nbytes((D_MODEL, tf), BF16) + 3 * _nbytes((tm, tf), F32) + 2 * _nbytes((tm, D_MODEL), F32))


def _mlp(x, gpre, wup, wdn, gpost, *, tm, tf):
    m_rows = x.shape[0]
    assert m_rows % tm == 0 and D_FF % tf == 0
    return pl.pallas_call(
        _mlp_kernel,
        grid=(m_rows // tm, D_FF // tf),
        in_specs=[pl.BlockSpec((tm, D_MODEL), lambda m, f: (m, 0)), pl.BlockSpec((1, D_MODEL), lambda m, f: (0, 0)),
                  pl.BlockSpec((D_MODEL, tf), lambda m, f: (0, f)), pl.BlockSpec((tf, D_MODEL), lambda m, f: (f, 0)),
                  pl.BlockSpec((1, D_MODEL), lambda m, f: (0, 0))],
        out_specs=pl.BlockSpec((tm, D_MODEL), lambda m, f: (m, 0)),
        out_shape=jax.ShapeDtypeStruct((m_rows, D_MODEL), F32),
        scratch_shapes=[pltpu.VMEM((tm, D_MODEL), BF16), pltpu.VMEM((tm, D_MODEL), F32)],
        compiler_params=_params(("parallel", "arbitrary"), _mlp_vmem(tm, tf)),
        name="mlp",
    )(x, gpre, wup, wdn, gpost)


def _decode_phases(s_idx, n_s, qm_ref, kn_ref, vn_ref, k_refs, v_refs, lq1_ref, lk1_ref, lq2_ref, lk2_ref, g_ref,
                   o_ref, m_scr, l_scr, acc_scr, *, dec_seq):
    n_rows = 2 * N_HEADS * dec_seq
    page_rows = PAGE_SIZE * N_HEADS

    def scores(kbs, bias):
        qm = qm_ref[0] * (HEAD_DIM ** -0.5)
        return [lax.dot_general(qm, kb, (((1,), (1,)), ((), ())), preferred_element_type=F32) + bias for kb in kbs]

    def absorb(ss, vbs):
        m_old = m_scr[...]
        m_new = m_old
        for s in ss:
            m_new = jnp.maximum(m_new, jnp.max(s, axis=1, keepdims=True))
        alpha = jnp.exp(m_old - m_new)
        l_new = alpha * l_scr[...]
        acc = alpha * acc_scr[...]
        for s, vb in zip(ss, vbs):
            p = jnp.exp(s - m_new)
            l_new = l_new + jnp.sum(p, axis=1, keepdims=True)
            acc = acc + jnp.dot(p, vb, preferred_element_type=F32)
        m_scr[...] = m_new
        l_scr[...] = l_new
        acc_scr[...] = acc

    def head_bias(width, extra=None):
        row = lax.broadcasted_iota(jnp.int32, (n_rows, width), 0)
        col = lax.broadcasted_iota(jnp.int32, (n_rows, width), 1)
        ok = (col & (N_HEADS - 1)) == row // (2 * dec_seq)
        if extra is not None:
            ok = ok & extra(row, col)
        return jnp.where(ok, 0.0, NEG)

    def pre():
        @pl.when(s_idx == 0)
        def _():
            m_scr[...] = jnp.full(m_scr.shape, NEG, F32)
            l_scr[...] = jnp.zeros(l_scr.shape, F32)
            acc_scr[...] = jnp.zeros(acc_scr.shape, F32)
            causal = lambda row, col: col // N_HEADS <= (row & (dec_seq - 1))
            absorb(scores([kn_ref[0]], head_bias(kn_ref.shape[1], causal)), [vn_ref[0]])

    def page_scores():
        return scores([r[0, 0].reshape(page_rows, V_DIM) for r in k_refs], head_bias(page_rows))

    def page_absorb(ss):
        absorb(ss, [r[0, 0].reshape(page_rows, V_DIM) for r in v_refs])

    def post():
        @pl.when(s_idx == n_s - 1)
        def _():
            lam = _diff_lambda(lq1_ref, lk1_ref, lq2_ref, lk2_ref)
            oa = acc_scr[...] / l_scr[...]
            d = oa - lam * pltpu.roll(oa, n_rows - dec_seq, axis=0)
            o_ref[0] = _rms(d, g_ref[...]) * (1.0 - LAMBDA_INIT)

    return pre, page_scores, page_absorb, post


def _mlp_decode_kernel(pt_ref, x_ref, gpre_ref, wup_ref, wdn_ref, gpost_ref, qm_ref, kn_ref, vn_ref, *rest,
                       n_pg, dec_seq, steps_per_seq):
    del pt_ref
    k_refs, v_refs = rest[:n_pg], rest[n_pg:2 * n_pg]
    (lq1_ref, lk1_ref, lq2_ref, lk2_ref, gsub_ref, out_ref, o_ref,
     hn_scr, acc_scr, u_scr, s_scr, m_scr, l_scr, dacc_scr) = rest[2 * n_pg:]
    f, n_f = pl.program_id(1), pl.num_programs(1)
    s_idx = lax.rem(pl.program_id(0) * n_f + f, steps_per_seq)
    mlp_pre, mlp_up, mlp_down, mlp_post = _mlp_phases(f, n_f, x_ref, gpre_ref, wup_ref, wdn_ref, gpost_ref, out_ref,
                                                      hn_scr, acc_scr)
    dec_pre, dec_scores, dec_absorb, dec_post = _decode_phases(
        s_idx, steps_per_seq, qm_ref, kn_ref, vn_ref, k_refs, v_refs, lq1_ref, lk1_ref, lq2_ref, lk2_ref, gsub_ref,
        o_ref, m_scr, l_scr, dacc_scr, dec_seq=dec_seq)
    mlp_pre()
    dec_pre()
    u_scr[...] = mlp_up()
    for p, s in enumerate(dec_scores()):
        s_scr[p] = s

    @pl.when(f >= 0)
    def _():
        mlp_down(u_scr[...])
        dec_absorb([s_scr[p] for p in range(n_pg)])

    mlp_post()
    dec_post()


def _mlp_decode(x, gpre, wup, wdn, gpost, qm, k_new, v_new, cache_k, cache_v, page_table, ap, *, tm, tf, dec_seq):
    m_rows = x.shape[0]
    b, n_pages = page_table.shape
    n_f = D_FF // tf
    n_steps = (m_rows // tm) * n_f
    assert m_rows % tm == 0 and D_FF % tf == 0 and n_steps % b == 0 and n_pages % (n_steps // b) == 0
    assert dec_seq & (dec_seq - 1) == 0 and N_HEADS & (N_HEADS - 1) == 0
    steps_per_seq = n_steps // b
    n_pg = n_pages // steps_per_seq
    n_rows = 2 * N_HEADS * dec_seq
    page_rows = PAGE_SIZE * N_HEADS
    seq_of = lambda m, f: (m * n_f + f) // steps_per_seq
    per_seq = lambda m, f, pt: (seq_of(m, f), 0, 0)
    const = lambda m, f, pt: (0, 0)
    page = lambda p: (lambda m, f, pt: (0, pt[seq_of(m, f), ((m * n_f + f) % steps_per_seq) * n_pg + p], 0, 0, 0))
    vmem = (_mlp_vmem(tm, tf) + 4 * n_pg * _nbytes((page_rows, V_DIM), F32)
            + 6 * n_pg * _nbytes((n_rows, page_rows), F32) + 8 * _nbytes((LANES, V_DIM), F32))
    grid_spec = pltpu.PrefetchScalarGridSpec(
        num_scalar_prefetch=1,
        grid=(m_rows // tm, n_f),
        in_specs=([pl.BlockSpec((tm, D_MODEL), lambda m, f, pt: (m, 0)), pl.BlockSpec((1, D_MODEL), const),
                   pl.BlockSpec((D_MODEL, tf), lambda m, f, pt: (0, f)),
                   pl.BlockSpec((tf, D_MODEL), lambda m, f, pt: (f, 0)), pl.BlockSpec((1, D_MODEL), const),
                   pl.BlockSpec((1, n_rows, V_DIM), per_seq), pl.BlockSpec((1, LANES, V_DIM), per_seq),
                   pl.BlockSpec((1, LANES, V_DIM), per_seq)]
                  + [pl.BlockSpec((1, 1, PAGE_SIZE, N_HEADS, V_DIM), page(p)) for p in range(n_pg)]
                  + [pl.BlockSpec((1, 1, PAGE_SIZE, N_HEADS, V_DIM), page(p)) for p in range(n_pg)]
                  + [pl.BlockSpec((1, HEAD_DIM), const)] * 4 + [pl.BlockSpec((1, V_DIM), const)]),
        out_specs=[pl.BlockSpec((tm, D_MODEL), lambda m, f, pt: (m, 0)), pl.BlockSpec((1, n_rows, V_DIM), per_seq)],
        scratch_shapes=[pltpu.VMEM((tm, D_MODEL), BF16), pltpu.VMEM((tm, D_MODEL), F32),
                        pltpu.VMEM((tm, tf), BF16), pltpu.VMEM((n_pg, n_rows, page_rows), F32),
                        pltpu.VMEM((n_rows, 1), F32), pltpu.VMEM((n_rows, 1), F32), pltpu.VMEM((n_rows, V_DIM), F32)],
    )
    return pl.pallas_call(
        functools.partial(_mlp_decode_kernel, n_pg=n_pg, dec_seq=dec_seq, steps_per_seq=steps_per_seq),
        grid_spec=grid_spec,
        out_shape=[jax.ShapeDtypeStruct((m_rows, D_MODEL), F32), jax.ShapeDtypeStruct((b, n_rows, V_DIM), F32)],
        compiler_params=_params(("arbitrary", "arbitrary"), vmem),
        name="mlp_attn_sample",
    )(page_table, x, gpre, wup, wdn, gpost, qm, k_new, v_new, *([cache_k] * n_pg), *([cache_v] * n_pg),
      ap["lam_q1"], ap["lam_k1"], ap["lam_q2"], ap["lam_k2"], ap["g_subln"])


_PROMPT_TILES = dict(in_tm=1024, in_tn=1024, lru_tt=512, attn_tq=256, attn_tk=512, merge_tm=256, mlp_tm=512, mlp_tf=512)
_SAMPLE_MLP_TF = 1024


def _component_queries(q, dec_seq):
    b = q.shape[0]
    qt = q.reshape(b, dec_seq, N_HEADS, 2, HEAD_DIM).transpose(0, 2, 3, 1, 4)
    onehot = jnp.eye(2, dtype=q.dtype)
    qm = qt[:, :, :, :, None, :] * onehot[None, None, :, None, :, None]
    return qm.reshape(b, 2 * N_HEADS * dec_seq, V_DIM)


def kernel(x_prompt, x_sample, cache_k, cache_v, state_lru, state_conv, page_table, meta_tokens, g_pre_mix, w_in,
           lam_q1, lam_k1, lam_q2, lam_k2, g_subln, w_attn_br, conv_w, conv_b, w_rg, b_rg, w_ig, b_ig, lru_lambda,
           w_lru_br, w_out, g_post_mix, g_pre_mlp, w_up, w_down, g_post_mlp):
    depth = w_in.shape[0]
    assert depth == 1
    batch, seq, _ = x_prompt.shape
    dec_batch, dec_seq, _ = x_sample.shape
    tl = _PROMPT_TILES

    w_in_b = w_in[0].astype(BF16)
    w_attn_b = w_attn_br[0].astype(BF16)
    w_lru_b = w_lru_br[0].astype(BF16)
    w_out_b = w_out[0].astype(BF16)
    w_up_b = w_up[0].astype(BF16)
    w_down_b = w_down[0].astype(BF16)
    lp = dict(conv_w=conv_w[0], conv_b=conv_b, w_rg=w_rg[0].astype(BF16), b_rg=b_rg, w_ig=w_ig[0].astype(BF16),
              b_ig=b_ig, lam=lru_lambda)
    ap = dict(lam_q1=lam_q1, lam_k1=lam_k1, lam_q2=lam_q2, lam_k2=lam_k2, g_subln=g_subln)

    n_small = N_META + dec_batch * dec_seq
    x_small = jnp.concatenate([meta_tokens, x_sample.reshape(dec_batch * dec_seq, D_MODEL)], axis=0)
    z_sm = _in_proj(x_small, g_pre_mix, w_in_b, tm=n_small, tn=tl["in_tn"])
    z_meta, z_s = z_sm[:N_META], z_sm[N_META:]

    zeros8 = jnp.zeros((1, SUBLANES, LRU_WIDTH), F32)
    _, h_meta, conv_meta = _lru(z_meta[None], z_meta[None], zeros8, jnp.zeros((1, 1, LRU_WIDTH), F32), lp,
                                tt=N_META, seq_start=True, t_valid=N_META,
                                xr_col=XR_COL // LRU_WIDTH, gr_col=GR_COL // LRU_WIDTH)
    k_meta, v_meta = z_meta[:, K_COL:K_COL + ATTN_WIDTH], z_meta[:, V_COL:V_COL + V_WIDTH]

    xp = x_prompt.reshape(batch * seq, D_MODEL)
    z_p = _in_proj(xp, g_pre_mix, w_in_b, tm=tl["in_tm"], tn=tl["in_tn"])
    z_p3 = z_p.reshape(batch, seq, IN_WIDTH)
    cbuf_p = jnp.broadcast_to(jnp.pad(conv_meta, ((0, 0), (SUBLANES - CONV_WIDTH + 1, 0), (0, 0))),
                              (batch, SUBLANES, LRU_WIDTH))
    hg_p, lru_prompt, conv_prompt = _lru(z_p3, z_p3, cbuf_p, jnp.broadcast_to(h_meta, (batch, 1, LRU_WIDTH)), lp,
                                         tt=tl["lru_tt"], seq_start=False, t_valid=seq,
                                         xr_col=XR_COL // LRU_WIDTH, gr_col=GR_COL // LRU_WIDTH)
    pad_meta = ((0, LANES - N_META), (0, 0))
    o_p, k_p, v_p = _attn_prompt(z_p3, jnp.pad(k_meta, pad_meta), jnp.pad(v_meta, pad_meta), ap,
                                 tq=tl["attn_tq"], tk=tl["attn_tk"])
    x1_p = _merge(o_p.reshape(batch * seq, V_WIDTH), hg_p.reshape(batch * seq, LRU_WIDTH), z_p, xp,
                  w_attn_b, w_lru_b, w_out_b, g_post_mix, tm=tl["merge_tm"], ga_col=GA_COL)

    n_s = dec_batch * dec_seq
    xs = x_sample.reshape(n_s, D_MODEL)
    q_s, k_s, v_s = z_s[:, Q_COL:Q_COL + ATTN_WIDTH], z_s[:, K_COL:K_COL + ATTN_WIDTH], z_s[:, V_COL:V_COL + V_WIDTH]
    pad_t = ((0, 0), (0, SUBLANES - dec_seq), (0, 0))
    cbuf_s = jnp.pad(state_conv[0], ((0, 0), (SUBLANES - CONV_WIDTH + 1, 0), (0, 0)))
    hg_s, lru_sample, conv_sample = _lru(
        jnp.pad(z_s[:, XR_COL:XR_COL + LRU_WIDTH].reshape(dec_batch, dec_seq, LRU_WIDTH), pad_t),
        jnp.pad(z_s[:, GR_COL:GR_COL + LRU_WIDTH].reshape(dec_batch, dec_seq, LRU_WIDTH), pad_t),
        cbuf_s, state_lru[0][:, None, :], lp, tt=SUBLANES, seq_start=False, t_valid=dec_seq)
    pad_new = ((0, 0), (0, LANES - dec_seq * N_HEADS), (0, 0))
    y_prompt, o_s = _mlp_decode(x1_p, g_pre_mlp, w_up_b, w_down_b, g_post_mlp,
                                _component_queries(q_s.reshape(dec_batch, dec_seq, ATTN_WIDTH), dec_seq),
                                jnp.pad(k_s.reshape(dec_batch, dec_seq * N_HEADS, V_DIM), pad_new),
                                jnp.pad(v_s.reshape(dec_batch, dec_seq * N_HEADS, V_DIM), pad_new),
                                cache_k, cache_v, page_table, ap, tm=tl["mlp_tm"], tf=tl["mlp_tf"], dec_seq=dec_seq)
    o_s = o_s.reshape(dec_batch, N_HEADS, 2, dec_seq, V_DIM)[:, :, 0].transpose(0, 2, 1, 3)
    o_s = o_s.reshape(n_s, V_WIDTH).astype(BF16)
    x1_s = _merge(o_s, hg_s[:, :dec_seq].reshape(n_s, LRU_WIDTH), z_s[:, GA_COL:], xs,
                  w_attn_b, w_lru_b, w_out_b, g_post_mix, tm=n_s, ga_col=0)
    y_sample = _mlp(x1_s, g_pre_mlp, w_up_b, w_down_b, g_post_mlp, tm=n_s, tf=_SAMPLE_MLP_TF)

    def with_meta(meta_rows, main, width):
        full = jnp.concatenate([jnp.broadcast_to(meta_rows[None], (batch, N_META, width)),
                                main.reshape(batch, seq, width)], axis=1)
        return full.reshape(1, batch, N_META + seq, N_HEADS, width // N_HEADS)

    return (y_prompt.reshape(batch, seq, D_MODEL),
            y_sample.reshape(dec_batch, dec_seq, D_MODEL),
            with_meta(k_meta, k_p, ATTN_WIDTH),
            with_meta(v_meta, v_p, V_WIDTH),
            lru_prompt.reshape(1, batch, LRU_WIDTH),
            conv_prompt.reshape(1, batch, CONV_WIDTH - 1, LRU_WIDTH),
            k_s.reshape(1, dec_batch, dec_seq, N_HEADS, 2 * HEAD_DIM),
            v_s.reshape(1, dec_batch, dec_seq, N_HEADS, V_DIM),
            lru_sample.reshape(1, dec_batch, LRU_WIDTH),
            conv_sample.reshape(1, dec_batch, CONV_WIDTH - 1, LRU_WIDTH))
```

```python
import functools
import math

import jax
import jax.numpy as jnp
from jax import lax
from jax.experimental import pallas as pl
from jax.experimental.pallas import tpu as pltpu

F32 = jnp.float32
BF16 = jnp.bfloat16

D_MODEL = 2048
N_META = 16
N_HEADS = 8
HEAD_DIM = 64
V_DIM = 2 * HEAD_DIM
ATTN_WIDTH = N_HEADS * 2 * HEAD_DIM
V_WIDTH = N_HEADS * V_DIM
LRU_WIDTH = D_MODEL // 2
N_LRU_BLOCKS = 8
LRU_BLOCK = LRU_WIDTH // N_LRU_BLOCKS
CONV_WIDTH = 4
LRU_C = 8.0
D_FF = 4 * D_MODEL
PAGE_SIZE = 128
NEG = -1e30
EPS = 1e-6
LAMBDA_INIT = 0.8 - 0.6 * math.exp(-0.3 * 0)
IN_WIDTH = 2 * ATTN_WIDTH + V_WIDTH + 2 * LRU_WIDTH + 2 * D_MODEL

SUBLANES = 8
LANES = 128
V7X_VMEM_CAP_BYTES = 60 * 1024 * 1024

Q_COL = 0
K_COL = Q_COL + ATTN_WIDTH
V_COL = K_COL + ATTN_WIDTH
XR_COL = V_COL + V_WIDTH
GR_COL = XR_COL + LRU_WIDTH
GA_COL = GR_COL + LRU_WIDTH
GL_COL = GA_COL + D_MODEL
assert GL_COL + D_MODEL == IN_WIDTH


def _nbytes(shape, dtype):
    return math.prod(shape) * jnp.dtype(dtype).itemsize


def _params(semantics, vmem_bytes):
    return pltpu.CompilerParams(dimension_semantics=semantics,
                                vmem_limit_bytes=min(int(vmem_bytes), V7X_VMEM_CAP_BYTES))


def _div_mod(x, n):
    if n & (n - 1) == 0:
        return x >> (n.bit_length() - 1), x & (n - 1)
    return x // n, x % n


def _rms(x, g):
    return x * lax.rsqrt(jnp.mean(x * x, axis=-1, keepdims=True) + EPS) * g


def _diff_lambda(lq1_ref, lk1_ref, lq2_ref, lk2_ref):
    a = jnp.sum(lq1_ref[...] * lk1_ref[...], axis=1, keepdims=True)
    b = jnp.sum(lq2_ref[...] * lk2_ref[...], axis=1, keepdims=True)
    return jnp.exp(a) - jnp.exp(b) + LAMBDA_INIT


def _in_proj_kernel(x_ref, g_ref, w_ref, z_ref, h_scr):
    @pl.when(pl.program_id(1) == 0)
    def _():
        h_scr[...] = _rms(x_ref[...], g_ref[...]).astype(BF16)

    z_ref[...] = jnp.dot(h_scr[...], w_ref[...], preferred_element_type=F32)


def _in_proj(x, g, w, *, tm, tn):
    m_rows = x.shape[0]
    assert m_rows % tm == 0 and IN_WIDTH % tn == 0
    vmem = (2 * _nbytes((tm, D_MODEL), F32) + _nbytes((tm, D_MODEL), BF16) + 2 * _nbytes((D_MODEL, tn), BF16)
            + 3 * _nbytes((tm, tn), F32) + _nbytes((tm, D_MODEL), F32))
    return pl.pallas_call(
        _in_proj_kernel,
        grid=(m_rows // tm, IN_WIDTH // tn),
        in_specs=[pl.BlockSpec((tm, D_MODEL), lambda m, n: (m, 0)),
                  pl.BlockSpec((1, D_MODEL), lambda m, n: (0, 0)),
                  pl.BlockSpec((D_MODEL, tn), lambda m, n: (0, n))],
        out_specs=pl.BlockSpec((tm, tn), lambda m, n: (m, n)),
        out_shape=jax.ShapeDtypeStruct((m_rows, IN_WIDTH), F32),
        scratch_shapes=[pltpu.VMEM((tm, D_MODEL), BF16)],
        compiler_params=_params(("parallel", "arbitrary"), vmem),
        name="in_proj",
    )(x, g, w)


def _lru_kernel(xr_ref, gr_ref, cbuf_ref, h0_ref, cw_ref, cb_ref, wrg_ref, brg_ref, wig_ref, big_ref, lam_ref,
                hg_ref, hlast_ref, cnew_ref, ext_scr, a_scr, u_scr, h_scr, *, tt, seq_start, t_last):
    c = pl.program_id(1)

    @pl.when(c == 0)
    def _():
        ext_scr[0:SUBLANES, :] = cbuf_ref[0]
        h_scr[...] = h0_ref[0]

    def rot8(x, s):
        return pltpu.roll(x.reshape(tt // SUBLANES, SUBLANES, LRU_WIDTH), s, axis=1).reshape(tt, LRU_WIDTH)

    row = lax.broadcasted_iota(jnp.int32, (tt, LRU_WIDTH), 0)
    rmod = row & (SUBLANES - 1)
    xr = xr_ref[0]
    ext_scr[SUBLANES:SUBLANES + tt, :] = xr
    x_prev8 = ext_scr[0:tt, :]
    cw = cw_ref[...]
    xc = cb_ref[...] + xr * cw[CONV_WIDTH - 1:CONV_WIDTH]
    for s in range(1, CONV_WIDTH):
        shifted = rot8(jnp.where(rmod >= SUBLANES - s, x_prev8, xr), s)
        xc = xc + shifted * cw[CONV_WIDTH - 1 - s:CONV_WIDTH - s]
    ext_scr[0:SUBLANES, :] = ext_scr[tt:tt + SUBLANES, :]

    xcb = xc.astype(BF16)

    def gate(w_ref, b_ref):
        parts = [jnp.dot(xcb[:, n * LRU_BLOCK:(n + 1) * LRU_BLOCK], w_ref[n], preferred_element_type=F32)
                 for n in range(N_LRU_BLOCKS)]
        return jax.nn.sigmoid(jnp.concatenate(parts, axis=1) + b_ref[...])

    r = gate(wrg_ref, brg_ref)
    i = gate(wig_ref, big_ref)
    lam = lam_ref[...]
    softplus_neg_lam = jnp.maximum(-lam, 0.0) + jnp.log1p(jnp.exp(-jnp.abs(lam)))
    log_a = (-LRU_C) * r * softplus_neg_lam
    a = jnp.exp(log_a)
    mult = jnp.sqrt(-jnp.tanh(log_a) * (a * a + 1.0))
    if seq_start:
        mult = jnp.where(row + c * tt == 0, 1.0, mult)
    u = mult * i * xc

    for s in (1, 2, 4):
        ok = rmod >= s
        a_prev = rot8(a, s)
        u_prev = rot8(u, s)
        u = jnp.where(ok, a * u_prev + u, u)
        a = jnp.where(ok, a * a_prev, a)
    a_scr[...] = a
    u_scr[...] = u

    def carry(gidx, h):
        rows = pl.ds(pl.multiple_of(gidx * SUBLANES, SUBLANES), SUBLANES)
        hrows = a_scr[rows, :] * h + u_scr[rows, :]
        u_scr[rows, :] = hrows
        return hrows[SUBLANES - 1:SUBLANES, :]

    h_scr[...] = lax.fori_loop(0, tt // SUBLANES, carry, h_scr[...])
    hg_ref[0] = (u_scr[...] * jax.nn.gelu(gr_ref[0])).astype(BF16)

    @pl.when(c == pl.num_programs(1) - 1)
    def _():
        hlast_ref[0] = u_scr[t_last:t_last + 1, :]
        cnew_ref[0] = ext_scr[SUBLANES + t_last - 2:SUBLANES + t_last + 1, :]


def _lru(xr, gr, cbuf, h0, lp, *, tt, seq_start, t_valid, xr_col=0, gr_col=0):
    b, t, _ = xr.shape
    w = LRU_WIDTH
    assert t % tt == 0 and tt % SUBLANES == 0 and (t_valid - 1) // tt == t // tt - 1 and t_valid >= CONV_WIDTH - 1
    t_last = (t_valid - 1) % tt
    assert t_last >= CONV_WIDTH - 2
    seq = lambda bi, c: (bi, c, 0)
    per_b = lambda bi, c: (bi, 0, 0)
    const2 = lambda bi, c: (0, 0)
    const3 = lambda bi, c: (0, 0, 0)
    vmem = (4 * _nbytes((tt, w), F32) + 2 * _nbytes((tt, w), BF16) + 3 * _nbytes((tt + SUBLANES, w), F32)
            + 12 * _nbytes((tt, w), F32) + 4 * _nbytes((N_LRU_BLOCKS, LRU_BLOCK, LRU_BLOCK), BF16))
    return pl.pallas_call(
        functools.partial(_lru_kernel, tt=tt, seq_start=seq_start, t_last=t_last),
        grid=(b, t // tt),
        in_specs=[pl.BlockSpec((1, tt, w), lambda bi, c: (bi, c, xr_col)),
                  pl.BlockSpec((1, tt, w), lambda bi, c: (bi, c, gr_col)),
                  pl.BlockSpec((1, SUBLANES, w), per_b), pl.BlockSpec((1, 1, w), per_b),
                  pl.BlockSpec((CONV_WIDTH, w), const2), pl.BlockSpec((1, w), const2),
                  pl.BlockSpec((N_LRU_BLOCKS, LRU_BLOCK, LRU_BLOCK), const3), pl.BlockSpec((1, w), const2),
                  pl.BlockSpec((N_LRU_BLOCKS, LRU_BLOCK, LRU_BLOCK), const3), pl.BlockSpec((1, w), const2),
                  pl.BlockSpec((1, w), const2)],
        out_specs=[pl.BlockSpec((1, tt, w), seq), pl.BlockSpec((1, 1, w), per_b),
                   pl.BlockSpec((1, CONV_WIDTH - 1, w), per_b)],
        out_shape=[jax.ShapeDtypeStruct((b, t, w), BF16), jax.ShapeDtypeStruct((b, 1, w), F32),
                   jax.ShapeDtypeStruct((b, CONV_WIDTH - 1, w), F32)],
        scratch_shapes=[pltpu.VMEM((tt + SUBLANES, w), F32), pltpu.VMEM((tt, w), F32), pltpu.VMEM((tt, w), F32),
                        pltpu.VMEM((1, w), F32)],
        compiler_params=_params(("parallel", "arbitrary"), vmem),
        name="conv_rglru",
    )(xr, gr, cbuf, h0, lp["conv_w"], lp["conv_b"], lp["w_rg"], lp["b_rg"], lp["w_ig"], lp["b_ig"], lp["lam"])


def _attn_kernel(q_ref, k_ref, v_ref, km_ref, vm_ref, lq1_ref, lk1_ref, lq2_ref, lk2_ref, g_ref, o_ref, ko_ref, vo_ref,
                 kb_scr, vt_scr, *, tq, tk):
    t = k_ref.shape[1]
    n_pad = km_ref.shape[0]
    ko_ref[0] = k_ref[0]
    vo_ref[0] = v_ref[0]
    kb_scr[0:t, :] = k_ref[0].astype(BF16)
    kb_scr[t:t + n_pad, :] = km_ref[...].astype(BF16)
    vt_scr[:, 0:t] = v_ref[0].T.astype(BF16)
    vt_scr[:, t:t + n_pad] = vm_ref[...].T.astype(BF16)
    lam = _diff_lambda(lq1_ref, lk1_ref, lq2_ref, lk2_ref)
    g = g_ref[...]
    lane = lax.broadcasted_iota(jnp.int32, (tq, V_DIM), 1)

    def update(state, qqt, kc, vtc, mask):
        m_old, l_old, acc = state
        st = jnp.dot(kc, qqt, preferred_element_type=F32)
        if mask is not None:
            st = jnp.where(mask, st, NEG)
        m_new = jnp.maximum(m_old, jnp.max(st, axis=0, keepdims=True))
        alpha = jnp.exp(m_old - m_new)
        p = jnp.exp(st - m_new)
        l_new = alpha * l_old + jnp.sum(p, axis=0, keepdims=True)
        acc = alpha * acc + jnp.dot(vtc, p.astype(BF16), preferred_element_type=F32)
        return m_new, l_new, acc

    for qi in range(t // tq):
        q = q_ref[0, qi * tq:(qi + 1) * tq, :] * (HEAD_DIM ** -0.5)
        qq = jnp.concatenate([jnp.where(lane < HEAD_DIM, q, 0.0), jnp.where(lane >= HEAD_DIM, q, 0.0)], axis=0)
        qqt = qq.T.astype(BF16)
        state = (jnp.full((1, 2 * tq), NEG, F32), jnp.zeros((1, 2 * tq), F32), jnp.zeros((V_DIM, 2 * tq), F32))
        n_keys = (qi + 1) * tq
        for ks in range(0, n_keys, tk):
            w = min(tk, n_keys - ks)
            if ks + w < n_keys:
                state = update(state, qqt, kb_scr[ks:ks + w, :], vt_scr[:, ks:ks + w], None)
                continue
            r = lax.broadcasted_iota(jnp.int32, (w + n_pad, 2 * tq), 0)
            qpos = qi * tq + (lax.broadcasted_iota(jnp.int32, (w + n_pad, 2 * tq), 1) & (tq - 1))
            key = jnp.where(r < w, ks + r, jnp.where(r < w + N_META, 0, t))
            kc = jnp.concatenate([kb_scr[ks:ks + w, :], kb_scr[t:t + n_pad, :]], axis=0)
            vtc = jnp.concatenate([vt_scr[:, ks:ks + w], vt_scr[:, t:t + n_pad]], axis=1)
            state = update(state, qqt, kc, vtc, key <= qpos)
        _, l_fin, acc = state
        oa = acc / l_fin
        ot = oa[:, 0:tq] - lam * oa[:, tq:2 * tq]
        o_ref[0, qi * tq:(qi + 1) * tq, :] = (_rms(ot.T, g) * (1.0 - LAMBDA_INIT)).astype(BF16)


def _attn_prompt(z, k_meta, v_meta, ap, *, tq, tk):
    b, t, _ = z.shape
    assert t % tq == 0 and tk % tq == 0 and tq & (tq - 1) == 0
    hd = lambda bi, h: (bi, 0, h)
    col = lambda c0: (lambda bi, h: (bi, 0, c0 // V_DIM + h))
    meta = lambda bi, h: (0, h)
    const = lambda bi, h: (0, 0)
    vmem = (10 * _nbytes((t, V_DIM), F32) + 2 * _nbytes((t, V_DIM), BF16) + 4 * _nbytes((LANES, V_DIM), F32)
            + 2 * _nbytes((t + LANES, V_DIM), BF16) + 2 * _nbytes((t, V_DIM), F32)
            + 8 * _nbytes((tk + LANES, 2 * tq), F32))
    return pl.pallas_call(
        functools.partial(_attn_kernel, tq=tq, tk=tk),
        grid=(b, N_HEADS),
        in_specs=[pl.BlockSpec((1, t, V_DIM), col(Q_COL)), pl.BlockSpec((1, t, V_DIM), col(K_COL)),
                  pl.BlockSpec((1, t, V_DIM), col(V_COL)),
                  pl.BlockSpec((LANES, V_DIM), meta), pl.BlockSpec((LANES, V_DIM), meta),
                  pl.BlockSpec((1, HEAD_DIM), const), pl.BlockSpec((1, HEAD_DIM), const),
                  pl.BlockSpec((1, HEAD_DIM), const), pl.BlockSpec((1, HEAD_DIM), const),
                  pl.BlockSpec((1, V_DIM), const)],
        out_specs=[pl.BlockSpec((1, t, V_DIM), hd)] * 3,
        out_shape=[jax.ShapeDtypeStruct((b, t, V_WIDTH), BF16), jax.ShapeDtypeStruct((b, t, ATTN_WIDTH), F32),
                   jax.ShapeDtypeStruct((b, t, V_WIDTH), F32)],
        scratch_shapes=[pltpu.VMEM((t + LANES, V_DIM), BF16), pltpu.VMEM((V_DIM, t + LANES), BF16)],
        compiler_params=_params(("parallel", "parallel"), vmem),
        name="attn_prompt",
    )(z, z, z, k_meta, v_meta, ap["lam_q1"], ap["lam_k1"], ap["lam_q2"], ap["lam_k2"], ap["g_subln"])


def _merge_kernel(o_ref, hg_ref, ga0_ref, ga1_ref, gl0_ref, gl1_ref, x_ref, wa_ref, wl_ref, wo_ref, g_ref, out_ref):
    ya = jnp.dot(o_ref[...], wa_ref[...], preferred_element_type=F32)
    yl = jnp.dot(hg_ref[...], wl_ref[...], preferred_element_type=F32)
    half = D_MODEL // 2
    merged = jnp.concatenate(
        [jax.nn.sigmoid(ga0_ref[...]) * ya[:, :half] + jax.nn.sigmoid(gl0_ref[...]) * yl[:, :half],
         jax.nn.sigmoid(ga1_ref[...]) * ya[:, half:] + jax.nn.sigmoid(gl1_ref[...]) * yl[:, half:]], axis=1)
    mo = jnp.dot(merged.astype(BF16), wo_ref[...], preferred_element_type=F32)
    out_ref[...] = x_ref[...] + _rms(mo, g_ref[...])


def _merge(o, hg, gates, x, wa, wl, wo, g, *, tm, ga_col):
    m_rows = x.shape[0]
    half = D_MODEL // 2
    assert m_rows % tm == 0 and ga_col % half == 0
    row = lambda m: (m, 0)
    const = lambda m: (0, 0)
    gate = lambda j: pl.BlockSpec((tm, half), lambda m: (m, ga_col // half + j))
    vmem = (2 * (2 * _nbytes((tm, V_WIDTH), BF16) + 4 * _nbytes((tm, D_MODEL), F32))
            + 2 * (2 * _nbytes((V_WIDTH, D_MODEL), BF16) + _nbytes((D_MODEL, D_MODEL), BF16))
            + 6 * _nbytes((tm, D_MODEL), F32))
    return pl.pallas_call(
        _merge_kernel,
        grid=(m_rows // tm,),
        in_specs=[pl.BlockSpec((tm, V_WIDTH), row), pl.BlockSpec((tm, LRU_WIDTH), row),
                  gate(0), gate(1), gate(2), gate(3),
                  pl.BlockSpec((tm, D_MODEL), row),
                  pl.BlockSpec((V_WIDTH, D_MODEL), const), pl.BlockSpec((LRU_WIDTH, D_MODEL), const),
                  pl.BlockSpec((D_MODEL, D_MODEL), const), pl.BlockSpec((1, D_MODEL), const)],
        out_specs=pl.BlockSpec((tm, D_MODEL), row),
        out_shape=jax.ShapeDtypeStruct((m_rows, D_MODEL), F32),
        compiler_params=_params(("parallel",), vmem),
        name="merge_out_proj",
    )(o, hg, gates, gates, gates, gates, x, wa, wl, wo, g)


def _mlp_phases(f, n_f, x_ref, gpre_ref, wup_ref, wdn_ref, gpost_ref, out_ref, hn_scr, acc_scr):
    def pre():
        @pl.when(f == 0)
        def _():
            hn_scr[...] = _rms(x_ref[...], gpre_ref[...]).astype(BF16)
            acc_scr[...] = jnp.zeros(acc_scr.shape, F32)

    def up():
        u = jnp.square(jnp.maximum(jnp.dot(hn_scr[...], wup_ref[...], preferred_element_type=F32), 0.0))
        return u.astype(BF16)

    def down(u):
        acc_scr[...] += jnp.dot(u, wdn_ref[...], preferred_element_type=F32)

    def post():
        @pl.when(f == n_f - 1)
        def _():
            out_ref[...] = x_ref[...] + _rms(acc_scr[...], gpost_ref[...])

    return pre, up, down, post


def _mlp_kernel(*refs):
    pre, up, down, post = _mlp_phases(pl.program_id(1), pl.num_programs(1), *refs)
    pre()
    down(up())
    post()


def _mlp_vmem(tm, tf):
    return (4 * _nbytes((tm, D_MODEL), F32) + _nbytes((tm, D_MODEL), BF16) + _nbytes((tm, D_MODEL), F32)
            + 4 * _nbytes((D_MODEL, tf), BF16) + 3 * _nbytes((tm, tf), F32) + 2 * _nbytes((tm, D_MODEL), F32))


def _mlp(x, gpre, wup, wdn, gpost, *, tm, tf):
    m_rows = x.shape[0]
    assert m_rows % tm == 0 and D_FF % tf == 0
    return pl.pallas_call(
        _mlp_kernel,
        grid=(m_rows // tm, D_FF // tf),
        in_specs=[pl.BlockSpec((tm, D_MODEL), lambda m, f: (m, 0)), pl.BlockSpec((1, D_MODEL), lambda m, f: (0, 0)),
                  pl.BlockSpec((D_MODEL, tf), lambda m, f: (0, f)), pl.BlockSpec((tf, D_MODEL), lambda m, f: (f, 0)),
                  pl.BlockSpec((1, D_MODEL), lambda m, f: (0, 0))],
        out_specs=pl.BlockSpec((tm, D_MODEL), lambda m, f: (m, 0)),
        out_shape=jax.ShapeDtypeStruct((m_rows, D_MODEL), F32),
        scratch_shapes=[pltpu.VMEM((tm, D_MODEL), BF16), pltpu.VMEM((tm, D_MODEL), F32)],
        compiler_params=_params(("parallel", "arbitrary"), _mlp_vmem(tm, tf)),
        name="mlp",
    )(x, gpre, wup, wdn, gpost)


def _decode_phases(s_idx, n_s, qm_ref, kn_ref, vn_ref, k_refs, v_refs, lq1_ref, lk1_ref, lq2_ref, lk2_ref, g_ref,
                   o_ref, m_scr, l_scr, acc_scr, *, dec_seq):
    n_rows = 2 * N_HEADS * dec_seq
    page_rows = PAGE_SIZE * N_HEADS

    def scores(kbs, bias):
        qm = qm_ref[0] * (HEAD_DIM ** -0.5)
        return [lax.dot_general(qm, kb, (((1,), (1,)), ((), ())), preferred_element_type=F32) + bias for kb in kbs]

    def absorb(ss, vbs):
        m_old = m_scr[...]
        m_new = m_old
        for s in ss:
            m_new = jnp.maximum(m_new, jnp.max(s, axis=1, keepdims=True))
        alpha = jnp.exp(m_old - m_new)
        l_new = alpha * l_scr[...]
        acc = alpha * acc_scr[...]
        for s, vb in zip(ss, vbs):
            p = jnp.exp(s - m_new)
            l_new = l_new + jnp.sum(p, axis=1, keepdims=True)
            acc = acc + jnp.dot(p, vb, preferred_element_type=F32)
        m_scr[...] = m_new
        l_scr[...] = l_new
        acc_scr[...] = acc

    def head_bias(width, extra=None):
        row = lax.broadcasted_iota(jnp.int32, (n_rows, width), 0)
        col = lax.broadcasted_iota(jnp.int32, (n_rows, width), 1)
        ok = (col & (N_HEADS - 1)) == row // (2 * dec_seq)
        if extra is not None:
            ok = ok & extra(row, col)
        return jnp.where(ok, 0.0, NEG)

    def pre():
        @pl.when(s_idx == 0)
        def _():
            m_scr[...] = jnp.full(m_scr.shape, NEG, F32)
            l_scr[...] = jnp.zeros(l_scr.shape, F32)
            acc_scr[...] = jnp.zeros(acc_scr.shape, F32)
            causal = lambda row, col: col // N_HEADS <= (row & (dec_seq - 1))
            absorb(scores([kn_ref[0]], head_bias(kn_ref.shape[1], causal)), [vn_ref[0]])

    def page_scores():
        return scores([r[0, 0].reshape(page_rows, V_DIM) for r in k_refs], head_bias(page_rows))

    def page_absorb(ss):
        absorb(ss, [r[0, 0].reshape(page_rows, V_DIM) for r in v_refs])

    def post():
        @pl.when(s_idx == n_s - 1)
        def _():
            lam = _diff_lambda(lq1_ref, lk1_ref, lq2_ref, lk2_ref)
            oa = acc_scr[...] / l_scr[...]
            d = oa - lam * pltpu.roll(oa, n_rows - dec_seq, axis=0)
            o_ref[0] = _rms(d, g_ref[...]) * (1.0 - LAMBDA_INIT)

    return pre, page_scores, page_absorb, post


def _mlp_decode_kernel(pt_ref, x_ref, gpre_ref, wup_ref, wdn_ref, gpost_ref, qm_ref, kn_ref, vn_ref, *rest,
                       n_pg, dec_seq, steps_per_seq):
    del pt_ref
    k_refs, v_refs = rest[:n_pg], rest[n_pg:2 * n_pg]
    (lq1_ref, lk1_ref, lq2_ref, lk2_ref, gsub_ref, out_ref, o_ref,
     hn_scr, acc_scr, u_scr, s_scr, m_scr, l_scr, dacc_scr) = rest[2 * n_pg:]
    f, n_f = pl.program_id(1), pl.num_programs(1)
    _, s_idx = _div_mod(pl.program_id(0) * n_f + f, steps_per_seq)
    mlp_pre, mlp_up, mlp_down, mlp_post = _mlp_phases(f, n_f, x_ref, gpre_ref, wup_ref, wdn_ref, gpost_ref, out_ref,
                                                      hn_scr, acc_scr)
    dec_pre, dec_scores, dec_absorb, dec_post = _decode_phases(
        s_idx, steps_per_seq, qm_ref, kn_ref, vn_ref, k_refs, v_refs, lq1_ref, lk1_ref, lq2_ref, lk2_ref, gsub_ref,
        o_ref, m_scr, l_scr, dacc_scr, dec_seq=dec_seq)
    mlp_pre()
    dec_pre()
    u_scr[...] = mlp_up()
    for p, s in enumerate(dec_scores()):
        s_scr[p] = s

    @pl.when(f >= 0)
    def _():
        mlp_down(u_scr[...])
        dec_absorb([s_scr[p] for p in range(n_pg)])

    mlp_post()
    dec_post()


def _mlp_decode(x, gpre, wup, wdn, gpost, qm, k_new, v_new, cache_k, cache_v, page_table, ap, *, tm, tf, dec_seq):
    m_rows = x.shape[0]
    b, n_pages = page_table.shape
    n_f = D_FF // tf
    n_steps = (m_rows // tm) * n_f
    assert m_rows % tm == 0 and D_FF % tf == 0 and n_steps % b == 0 and n_pages % (n_steps // b) == 0
    assert dec_seq & (dec_seq - 1) == 0 and N_HEADS & (N_HEADS - 1) == 0
    steps_per_seq = n_steps // b
    n_pg = n_pages // steps_per_seq
    n_rows = 2 * N_HEADS * dec_seq
    page_rows = PAGE_SIZE * N_HEADS
    seq_step = lambda m, f: _div_mod(m * n_f + f, steps_per_seq)
    per_seq = lambda m, f, pt: (seq_step(m, f)[0], 0, 0)
    const = lambda m, f, pt: (0, 0)

    def page(p):
        def index(m, f, pt):
            seq, step = seq_step(m, f)
            return 0, pt[seq, step * n_pg + p], 0, 0, 0
        return index

    vmem = (_mlp_vmem(tm, tf) + 4 * n_pg * _nbytes((page_rows, V_DIM), F32)
            + 6 * n_pg * _nbytes((n_rows, page_rows), F32) + 8 * _nbytes((LANES, V_DIM), F32))
    grid_spec = pltpu.PrefetchScalarGridSpec(
        num_scalar_prefetch=1,
        grid=(m_rows // tm, n_f),
        in_specs=([pl.BlockSpec((tm, D_MODEL), lambda m, f, pt: (m, 0)), pl.BlockSpec((1, D_MODEL), const),
                   pl.BlockSpec((D_MODEL, tf), lambda m, f, pt: (0, f)),
                   pl.BlockSpec((tf, D_MODEL), lambda m, f, pt: (f, 0)), pl.BlockSpec((1, D_MODEL), const),
                   pl.BlockSpec((1, n_rows, V_DIM), per_seq), pl.BlockSpec((1, LANES, V_DIM), per_seq),
                   pl.BlockSpec((1, LANES, V_DIM), per_seq)]
                  + [pl.BlockSpec((1, 1, PAGE_SIZE, N_HEADS, V_DIM), page(p)) for p in range(n_pg)]
                  + [pl.BlockSpec((1, 1, PAGE_SIZE, N_HEADS, V_DIM), page(p)) for p in range(n_pg)]
                  + [pl.BlockSpec((1, HEAD_DIM), const)] * 4 + [pl.BlockSpec((1, V_DIM), const)]),
        out_specs=[pl.BlockSpec((tm, D_MODEL), lambda m, f, pt: (m, 0)), pl.BlockSpec((1, n_rows, V_DIM), per_seq)],
        scratch_shapes=[pltpu.VMEM((tm, D_MODEL), BF16), pltpu.VMEM((tm, D_MODEL), F32),
                        pltpu.VMEM((tm, tf), BF16), pltpu.VMEM((n_pg, n_rows, page_rows), F32),
                        pltpu.VMEM((n_rows, 1), F32), pltpu.VMEM((n_rows, 1), F32), pltpu.VMEM((n_rows, V_DIM), F32)],
    )
    return pl.pallas_call(
        functools.partial(_mlp_decode_kernel, n_pg=n_pg, dec_seq=dec_seq, steps_per_seq=steps_per_seq),
        grid_spec=grid_spec,
        out_shape=[jax.ShapeDtypeStruct((m_rows, D_MODEL), F32), jax.ShapeDtypeStruct((b, n_rows, V_DIM), F32)],
        compiler_params=_params(("arbitrary", "arbitrary"), vmem),
        name="mlp_attn_sample",
    )(page_table, x, gpre, wup, wdn, gpost, qm, k_new, v_new, *([cache_k] * n_pg), *([cache_v] * n_pg),
      ap["lam_q1"], ap["lam_k1"], ap["lam_q2"], ap["lam_k2"], ap["g_subln"])


_PROMPT_TILES = dict(in_tm=1024, in_tn=1024, lru_tt=512, attn_tq=256, attn_tk=1024, merge_tm=256, mlp_tm=512, mlp_tf=512)
_SAMPLE_MLP_TF = 1024


def _component_queries(q, dec_seq):
    b = q.shape[0]
    qt = q.reshape(b, dec_seq, N_HEADS, 2, HEAD_DIM).transpose(0, 2, 3, 1, 4)
    onehot = jnp.eye(2, dtype=q.dtype)
    qm = qt[:, :, :, :, None, :] * onehot[None, None, :, None, :, None]
    return qm.reshape(b, 2 * N_HEADS * dec_seq, V_DIM)


def kernel(x_prompt, x_sample, cache_k, cache_v, state_lru, state_conv, page_table, meta_tokens, g_pre_mix, w_in,
           lam_q1, lam_k1, lam_q2, lam_k2, g_subln, w_attn_br, conv_w, conv_b, w_rg, b_rg, w_ig, b_ig, lru_lambda,
           w_lru_br, w_out, g_post_mix, g_pre_mlp, w_up, w_down, g_post_mlp):
    depth = w_in.shape[0]
    assert depth == 1
    batch, seq, _ = x_prompt.shape
    dec_batch, dec_seq, _ = x_sample.shape
    tl = _PROMPT_TILES

    w_in_b = w_in[0].astype(BF16)
    w_attn_b = w_attn_br[0].astype(BF16)
    w_lru_b = w_lru_br[0].astype(BF16)
    w_out_b = w_out[0].astype(BF16)
    w_up_b = w_up[0].astype(BF16)
    w_down_b = w_down[0].astype(BF16)
    lp = dict(conv_w=conv_w[0], conv_b=conv_b, w_rg=w_rg[0].astype(BF16), b_rg=b_rg, w_ig=w_ig[0].astype(BF16),
              b_ig=b_ig, lam=lru_lambda)
    ap = dict(lam_q1=lam_q1, lam_k1=lam_k1, lam_q2=lam_q2, lam_k2=lam_k2, g_subln=g_subln)

    n_small = N_META + dec_batch * dec_seq
    x_small = jnp.concatenate([meta_tokens, x_sample.reshape(dec_batch * dec_seq, D_MODEL)], axis=0)
    z_sm = _in_proj(x_small, g_pre_mix, w_in_b, tm=n_small, tn=tl["in_tn"])
    z_meta, z_s = z_sm[:N_META], z_sm[N_META:]

    zeros8 = jnp.zeros((1, SUBLANES, LRU_WIDTH), F32)
    _, h_meta, conv_meta = _lru(z_meta[None], z_meta[None], zeros8, jnp.zeros((1, 1, LRU_WIDTH), F32), lp,
                                tt=N_META, seq_start=True, t_valid=N_META,
                                xr_col=XR_COL // LRU_WIDTH, gr_col=GR_COL // LRU_WIDTH)
    k_meta, v_meta = z_meta[:, K_COL:K_COL + ATTN_WIDTH], z_meta[:, V_COL:V_COL + V_WIDTH]

    xp = x_prompt.reshape(batch * seq, D_MODEL)
    z_p = _in_proj(xp, g_pre_mix, w_in_b, tm=tl["in_tm"], tn=tl["in_tn"])
    z_p3 = z_p.reshape(batch, seq, IN_WIDTH)
    cbuf_p = jnp.broadcast_to(jnp.pad(conv_meta, ((0, 0), (SUBLANES - CONV_WIDTH + 1, 0), (0, 0))),
                              (batch, SUBLANES, LRU_WIDTH))
    hg_p, lru_prompt, conv_prompt = _lru(z_p3, z_p3, cbuf_p, jnp.broadcast_to(h_meta, (batch, 1, LRU_WIDTH)), lp,
                                         tt=tl["lru_tt"], seq_start=False, t_valid=seq,
                                         xr_col=XR_COL // LRU_WIDTH, gr_col=GR_COL // LRU_WIDTH)
    pad_meta = ((0, LANES - N_META), (0, 0))
    o_p, k_p, v_p = _attn_prompt(z_p3, jnp.pad(k_meta, pad_meta), jnp.pad(v_meta, pad_meta), ap,
                                 tq=tl["attn_tq"], tk=tl["attn_tk"])
    x1_p = _merge(o_p.reshape(batch * seq, V_WIDTH), hg_p.reshape(batch * seq, LRU_WIDTH), z_p, xp,
                  w_attn_b, w_lru_b, w_out_b, g_post_mix, tm=tl["merge_tm"], ga_col=GA_COL)

    n_s = dec_batch * dec_seq
    xs = x_sample.reshape(n_s, D_MODEL)
    q_s, k_s, v_s = z_s[:, Q_COL:Q_COL + ATTN_WIDTH], z_s[:, K_COL:K_COL + ATTN_WIDTH], z_s[:, V_COL:V_COL + V_WIDTH]
    pad_t = ((0, 0), (0, SUBLANES - dec_seq), (0, 0))
    cbuf_s = jnp.pad(state_conv[0], ((0, 0), (SUBLANES - CONV_WIDTH + 1, 0), (0, 0)))
    hg_s, lru_sample, conv_sample = _lru(
        jnp.pad(z_s[:, XR_COL:XR_COL + LRU_WIDTH].reshape(dec_batch, dec_seq, LRU_WIDTH), pad_t),
        jnp.pad(z_s[:, GR_COL:GR_COL + LRU_WIDTH].reshape(dec_batch, dec_seq, LRU_WIDTH), pad_t),
        cbuf_s, state_lru[0][:, None, :], lp, tt=SUBLANES, seq_start=False, t_valid=dec_seq)
    pad_new = ((0, 0), (0, LANES - dec_seq * N_HEADS), (0, 0))
    y_prompt, o_s = _mlp_decode(x1_p, g_pre_mlp, w_up_b, w_down_b, g_post_mlp,
                                _component_queries(q_s.reshape(dec_batch, dec_seq, ATTN_WIDTH), dec_seq),
                                jnp.pad(k_s.reshape(dec_batch, dec_seq * N_HEADS, V_DIM), pad_new),
                                jnp.pad(v_s.reshape(dec_batch, dec_seq * N_HEADS, V_DIM), pad_new),
                                cache_k, cache_v, page_table, ap, tm=tl["mlp_tm"], tf=tl["mlp_tf"], dec_seq=dec_seq)
    o_s = o_s.reshape(dec_batch, N_HEADS, 2, dec_seq, V_DIM)[:, :, 0].transpose(0, 2, 1, 3)
    o_s = o_s.reshape(n_s, V_WIDTH).astype(BF16)
    x1_s = _merge(o_s, hg_s[:, :dec_seq].reshape(n_s, LRU_WIDTH), z_s[:, GA_COL:], xs,
                  w_attn_b, w_lru_b, w_out_b, g_post_mix, tm=n_s, ga_col=0)
    y_sample = _mlp(x1_s, g_pre_mlp, w_up_b, w_down_b, g_post_mlp, tm=n_s, tf=_SAMPLE_MLP_TF)

    def with_meta(meta_rows, main, width):
        full = jnp.concatenate([jnp.broadcast_to(meta_rows[None], (batch, N_META, width)),
                                main.reshape(batch, seq, width)], axis=1)
        return full.reshape(1, batch, N_META + seq, N_HEADS, width // N_HEADS)

    return (y_prompt.reshape(batch, seq, D_MODEL),
            y_sample.reshape(dec_batch, dec_seq, D_MODEL),
            with_meta(k_meta, k_p, ATTN_WIDTH),
            with_meta(v_meta, v_p, V_WIDTH),
            lru_prompt.reshape(1, batch, LRU_WIDTH),
            conv_prompt.reshape(1, batch, CONV_WIDTH - 1, LRU_WIDTH),
            k_s.reshape(1, dec_batch, dec_seq, N_HEADS, 2 * HEAD_DIM),
            v_s.reshape(1, dec_batch, dec_seq, N_HEADS, V_DIM),
            lru_sample.reshape(1, dec_batch, LRU_WIDTH),
            conv_sample.reshape(1, dec_batch, CONV_WIDTH - 1, LRU_WIDTH))
```
